```python
import math
import jax, jax.numpy as jnp
from jax import lax
import numpy as np

D_MODEL = 1024
BATCH = 8
SEQ = 2048
DEPTH = 4

N_MIXERS = 2
N_REC_LAYERS = (DEPTH + 1) // 2
N_ATT_LAYERS = DEPTH // 2
EPS = 1e-6

D_RNN = D_MODEL
RG_BLOCKS = 4
RG_BW = D_RNN // RG_BLOCKS
CONV_W = 4
RG_C = 8.0

H_ATT = 16
D_QK = 64
D_V = 64
D_LAT = 128
H_IDX = 8
D_IDX = 64
TOPK_MAX = 256
Q_BLOCK = 128
ATT_SCALE = D_QK ** -0.5
IDX_W_SCALE = (H_IDX ** -0.5) * (D_IDX ** -0.5)
ATT_IN_SPLITS = [H_ATT * D_QK,
                 H_ATT * D_QK + D_LAT,
                 H_ATT * D_QK + D_LAT + H_IDX * D_IDX,
                 H_ATT * D_QK + D_LAT + H_IDX * D_IDX + D_IDX]
ATT_IN_DIM = ATT_IN_SPLITS[-1] + H_IDX

NUM_BUCKETS = 32
MAX_DISTANCE = 128
MAX_EXACT = NUM_BUCKETS // 2

D_FF = 4 * D_MODEL

kernel_name = "hybrid_rglru_dsa_sqrelu_trunk"


def rms_norm(x, g):
    xf = x.astype(jnp.float32)
    y = xf * lax.rsqrt(jnp.mean(xf * xf, axis=-1, keepdims=True) + EPS)
    return (y * g.astype(jnp.float32)).astype(x.dtype)


def t5_bucket(dist):
    n = jnp.maximum(dist, 0)
    nf = jnp.maximum(n, 1).astype(jnp.float32)
    large = MAX_EXACT + (jnp.log(nf / MAX_EXACT) / math.log(MAX_DISTANCE / MAX_EXACT)
                         * (NUM_BUCKETS - MAX_EXACT)).astype(jnp.int32)
    large = jnp.minimum(large, NUM_BUCKETS - 1)
    return jnp.where(n < MAX_EXACT, n, large)


def rglru_mixer(h, w_in, conv_w, conv_b, w_a, b_a, w_x, b_x, lam, w_out):
    B, S, _ = h.shape
    gate, xr = jnp.split(h @ w_in, 2, axis=-1)
    gate = jax.nn.gelu(gate)
    xpad = jnp.pad(xr, ((0, 0), (CONV_W - 1, 0), (0, 0)))
    xc = conv_b + xpad[:, 0:S] * conv_w[0]
    for k in range(1, CONV_W):
        xc = xc + xpad[:, k:k + S] * conv_w[k]
    xb = xc.reshape(B, S, RG_BLOCKS, RG_BW)
    r = jax.nn.sigmoid(jnp.einsum('bsnj,nji->bsni', xb, w_a).reshape(B, S, D_RNN) + b_a)
    i = jax.nn.sigmoid(jnp.einsum('bsnj,nji->bsni', xb, w_x).reshape(B, S, D_RNN) + b_x)
    log_a = (-RG_C * r.astype(jnp.float32)) * jax.nn.softplus(-lam.astype(jnp.float32))
    a = jnp.exp(log_a)
    u = jnp.sqrt(-jnp.expm1(2.0 * log_a)) * (i * xc).astype(jnp.float32)

    def combine(left, right):
        a_l, b_l = left
        a_r, b_r = right
        return a_l * a_r, a_r * b_l + b_r

    _, hs = lax.associative_scan(combine, (a, u), axis=1)
    y = hs.astype(h.dtype) * gate
    return y @ w_out


def dsa_mixer(h, w_in, kv_norm_g, w_uk, w_uv, w_o, rel_bias):
    B, S, _ = h.shape
    n_blk = S // Q_BLOCK
    topk = min(TOPK_MAX, S // 4)
    q, c, qi, ki, wi = jnp.split(h @ w_in, ATT_IN_SPLITS, axis=-1)
    q = q.reshape(B, S, H_ATT, D_QK)
    c = rms_norm(c, kv_norm_g)
    qi = qi.reshape(B, S, H_IDX, D_IDX)
    wi = wi * IDX_W_SCALE
    q_lat = jnp.einsum('bshd,hdc->bshc', q, w_uk) * ATT_SCALE
    key_pos = jnp.arange(S)

    def block(args):
        q_lat_b, qi_b, wi_b, pos_b = args
        dots = jax.nn.relu(jnp.einsum('bqhd,bsd->bqhs', qi_b, ki))
        score = jnp.einsum('bqh,bqhs->bqs', wi_b, dots).astype(jnp.float32)
        causal = key_pos[None, :] <= pos_b[:, None]
        score = jnp.where(causal[None], score, -jnp.inf)
        _, idx = lax.top_k(score, topk)
        c_sel = jax.vmap(lambda cb, ib: cb[ib])(c, idx)
        dist = pos_b[None, :, None] - idx
        bias = jnp.moveaxis(rel_bias[t5_bucket(dist)], -1, 2)
        logits = jnp.einsum('bqhc,bqkc->bqhk', q_lat_b, c_sel).astype(jnp.float32) \
            + bias.astype(jnp.float32)
        logits = jnp.where((dist >= 0)[:, :, None, :], logits, -jnp.inf)
        p = jax.nn.softmax(logits, axis=-1).astype(c.dtype)
        return jnp.einsum('bqhk,bqkc->bqhc', p, c_sel)

    def to_blocks(a):
        return a.reshape(B, n_blk, Q_BLOCK, *a.shape[2:]).swapaxes(0, 1)

    o_lat = lax.map(block, (to_blocks(q_lat), to_blocks(qi), to_blocks(wi),
                            key_pos.reshape(n_blk, Q_BLOCK)))
    o_lat = o_lat.swapaxes(0, 1).reshape(B, S, H_ATT, D_LAT)
    o = jnp.einsum('bshc,hcv->bshv', o_lat, w_uv).reshape(B, S, H_ATT * D_V)
    return o @ w_o


def sq_relu_mlp(h, w_up, w_down):
    return jnp.square(jax.nn.relu(h @ w_up)) @ w_down


def setup_inputs(seed: int = 0) -> dict:
    key = jax.random.key(seed)
    ks = jax.random.split(key, 24)
    nr, na, L = N_REC_LAYERS, N_ATT_LAYERS, DEPTH
    f32 = jnp.float32

    def nrm(k, shape, scale):
        return jax.random.normal(k, shape, f32) * scale

    u = jax.random.uniform(ks[10], (nr, D_RNN), f32, 0.9, 0.999)
    return {
        "x": jax.random.normal(ks[0], (BATCH, SEQ, D_MODEL), f32),
        "norm_mix_g": 1.0 + nrm(ks[1], (L, D_MODEL), 0.02),
        "norm_mlp_g": 1.0 + nrm(ks[2], (L, D_MODEL), 0.02),
        "final_norm_g": 1.0 + nrm(ks[3], (D_MODEL,), 0.02),
        "rec_w_in": nrm(ks[4], (nr, D_MODEL, 2 * D_RNN), D_MODEL ** -0.5),
        "rec_conv_w": nrm(ks[5], (nr, CONV_W, D_RNN), CONV_W ** -0.5),
        "rec_conv_b": nrm(ks[6], (nr, D_RNN), 0.01),
        "rec_w_a": nrm(ks[7], (nr, RG_BLOCKS, RG_BW, RG_BW), RG_BW ** -0.5),
        "rec_b_a": nrm(ks[8], (nr, D_RNN), 0.01),
        "rec_w_x": nrm(ks[9], (nr, RG_BLOCKS, RG_BW, RG_BW), RG_BW ** -0.5),
        "rec_b_x": nrm(ks[11], (nr, D_RNN), 0.01),
        "rec_lambda": jnp.log(u) - jnp.log1p(-u),
        "rec_w_out": nrm(ks[12], (nr, D_RNN, D_MODEL), D_RNN ** -0.5),
        "att_w_in": nrm(ks[13], (na, D_MODEL, ATT_IN_DIM), D_MODEL ** -0.5),
        "att_kv_norm_g": 1.0 + nrm(ks[14], (na, D_LAT), 0.02),
        "att_w_uk": nrm(ks[15], (na, H_ATT, D_QK, D_LAT), D_LAT ** -0.5),
        "att_w_uv": nrm(ks[16], (na, H_ATT, D_LAT, D_V), D_LAT ** -0.5),
        "att_w_o": nrm(ks[17], (na, H_ATT * D_V, D_MODEL), (H_ATT * D_V) ** -0.5),
        "rel_bias": nrm(ks[18], (NUM_BUCKETS, H_ATT), 0.2),
        "mlp_w_up": nrm(ks[19], (L, D_MODEL, D_FF), D_MODEL ** -0.5),
        "mlp_w_down": nrm(ks[20], (L, D_FF, D_MODEL), D_FF ** -0.5),
    }


def reference(x, norm_mix_g, norm_mlp_g, final_norm_g,
              rec_w_in, rec_conv_w, rec_conv_b, rec_w_a, rec_b_a, rec_w_x, rec_b_x,
              rec_lambda, rec_w_out,
              att_w_in, att_kv_norm_g, att_w_uk, att_w_uv, att_w_o, rel_bias,
              mlp_w_up, mlp_w_down):
    for layer in range(DEPTH):
        j = layer // N_MIXERS
        hn = rms_norm(x, norm_mix_g[layer])
        if layer % N_MIXERS == 0:
            mix = rglru_mixer(hn, rec_w_in[j], rec_conv_w[j], rec_conv_b[j],
                              rec_w_a[j], rec_b_a[j], rec_w_x[j], rec_b_x[j],
                              rec_lambda[j], rec_w_out[j])
        else:
            mix = dsa_mixer(hn, att_w_in[j], att_kv_norm_g[j], att_w_uk[j],
                            att_w_uv[j], att_w_o[j], rel_bias)
        x = x + mix
        x = x + sq_relu_mlp(rms_norm(x, norm_mlp_g[layer]), mlp_w_up[layer], mlp_w_down[layer])
    return rms_norm(x, final_norm_g)
```

```python
import functools
import math

import jax
import jax.numpy as jnp
from jax import lax
from jax.experimental import pallas as pl
from jax.experimental.pallas import tpu as pltpu

F32 = jnp.float32
BF16 = jnp.bfloat16

EPS = 1e-6
RG_BLOCKS = 4
CONV_W = 4
RG_C = 8.0
H_ATT = 16
D_QK = 64
D_V = 64
D_LAT = 128
H_IDX = 8
D_IDX = 64
TOPK_MAX = 256
ATT_SCALE = D_QK ** -0.5
IDX_W_SCALE = (H_IDX ** -0.5) * (D_IDX ** -0.5)
NUM_BUCKETS = 32
MAX_DISTANCE = 128
MAX_EXACT = NUM_BUCKETS // 2

LANES = 128
SUBLANES = 8
VMEM_LIMIT_BYTES = 56 * 1024 * 1024

Q_TILE = 128
K_TILE = 256
N_BISECT = 18
NEG_INF = float("-inf")
POS_INF = float("inf")


def _rms(x, g):
    ms = jnp.mean(x * x, axis=-1, keepdims=True)
    return x * lax.rsqrt(ms + EPS) * g


def _dot(a, b):
    return jnp.dot(a, b, preferred_element_type=F32)


def _dot_nt(a, b):
    return lax.dot_general(a, b, (((1,), (1,)), ((), ())), preferred_element_type=F32)


def _const_spec(shape):
    nd = len(shape)
    return pl.BlockSpec(shape, lambda *_: (0,) * nd, pipeline_mode=pl.Buffered(1))


def _mlp_body(x_ref, y_ref, wpre_ref, g_ref, wup_ref, wdn_ref, gfin_ref, o_ref, *, ff_chunk, final):
    x1 = x_ref[...] + _dot(y_ref[...], wpre_ref[...])
    hn = _rms(x1, g_ref[...]).astype(BF16)
    acc = x1
    d_ff = wup_ref.shape[1]
    for c in range(d_ff // ff_chunk):
        h = _dot(hn, wup_ref[:, c * ff_chunk:(c + 1) * ff_chunk])
        h = jnp.maximum(h, 0.0)
        h = (h * h).astype(BF16)
        acc = acc + _dot(h, wdn_ref[c * ff_chunk:(c + 1) * ff_chunk, :])
    if final:
        acc = _rms(acc, gfin_ref[...])
    o_ref[...] = acc


def _mlp_layer(x2d, y2d, w_pre, g, w_up, w_dn, g_fin, *, final, tm=512, ff_chunk=512):
    n, d = x2d.shape
    d_ff = w_up.shape[1]
    tm = min(tm, n)
    return pl.pallas_call(
        functools.partial(_mlp_body, ff_chunk=min(ff_chunk, d_ff), final=final),
        grid=(n // tm,),
        in_specs=[
            pl.BlockSpec((tm, d), lambda i: (i, 0)),
            pl.BlockSpec((tm, d), lambda i: (i, 0)),
            _const_spec((d, d)),
            _const_spec((1, d)),
            _const_spec((d, d_ff)),
            _const_spec((d_ff, d)),
            _const_spec((1, d)),
        ],
        out_specs=pl.BlockSpec((tm, d), lambda i: (i, 0)),
        out_shape=jax.ShapeDtypeStruct((n, d), F32),
        compiler_params=pltpu.CompilerParams(
            dimension_semantics=("arbitrary",), vmem_limit_bytes=VMEM_LIMIT_BYTES),
        name="mlp_block",
    )(x2d, y2d, w_pre, g, w_up, w_dn, g_fin)


def _gelu_tanh(x):
    c = math.sqrt(2.0 / math.pi)
    return 0.5 * x * (1.0 + jnp.tanh(c * (x + 0.044715 * (x * x * x))))


def _sigmoid(x):
    return 1.0 / (1.0 + jnp.exp(-x))


def _rec_body(x_ref, g_ref, win_ref, cw_ref, cb_ref, wa_ref, ba_ref, wx_ref, bx_ref, lam_ref,
              y_ref, xbuf, a_s, u_s, h_s, hcar, *, ts):
    d = x_ref.shape[-1]
    bw = d // RG_BLOCKS

    @pl.when(pl.program_id(1) == 0)
    def _():
        xbuf[0:SUBLANES, :] = jnp.zeros((SUBLANES, d), F32)
        hcar[...] = jnp.zeros((1, d), F32)

    hn = _rms(x_ref[0], g_ref[...]).astype(BF16)
    y2 = _dot(hn, win_ref[...])
    gate = _gelu_tanh(y2[:, :d])
    xr = y2[:, d:]

    xbuf[SUBLANES:SUBLANES + ts, :] = xr
    cw = cw_ref[...]
    xc = cb_ref[...] + xbuf[pl.ds(SUBLANES - 3, ts), :] * cw[0:1]
    xc = xc + xbuf[pl.ds(SUBLANES - 2, ts), :] * cw[1:2]
    xc = xc + xbuf[pl.ds(SUBLANES - 1, ts), :] * cw[2:3]
    xc = xc + xr * cw[3:4]
    xbuf[0:SUBLANES, :] = xbuf[ts:ts + SUBLANES, :]

    xcb = xc.astype(BF16)
    ra = jnp.concatenate(
        [_dot(xcb[:, n * bw:(n + 1) * bw], wa_ref[n]) for n in range(RG_BLOCKS)], axis=1)
    rx = jnp.concatenate(
        [_dot(xcb[:, n * bw:(n + 1) * bw], wx_ref[n]) for n in range(RG_BLOCKS)], axis=1)
    r = _sigmoid(ra + ba_ref[...])
    ig = _sigmoid(rx + bx_ref[...])
    nl = -lam_ref[...]
    softplus = jnp.maximum(nl, 0.0) + jnp.log1p(jnp.exp(-jnp.abs(nl)))
    log_a = (-RG_C * r) * softplus
    a = jnp.exp(log_a)
    u = jnp.sqrt(-jnp.tanh(log_a) * (a * a + 1.0)) * (ig * xc)
    a_s[...] = a
    u_s[...] = u

    row = lax.broadcasted_iota(jnp.int32, (SUBLANES, d), 0)
    masks = [(s, row >= s) for s in (1, 2, 4)]

    def chunk(c, hp):
        r0 = pl.multiple_of(c * SUBLANES, SUBLANES)
        av = a_s[pl.ds(r0, SUBLANES), :]
        uv = u_s[pl.ds(r0, SUBLANES), :]
        for s, m in masks:
            a_sh = jnp.where(m, pltpu.roll(av, s, 0), 1.0)
            u_sh = jnp.where(m, pltpu.roll(uv, s, 0), 0.0)
            uv = av * u_sh + uv
            av = av * a_sh
        hv = av * hp + uv
        h_s[pl.ds(r0, SUBLANES), :] = hv
        return hv[SUBLANES - 1:SUBLANES, :]

    hcar[...] = lax.fori_loop(0, ts // SUBLANES, chunk, hcar[...])
    y_ref[0] = (h_s[...] * gate).astype(BF16)


def _rec_mixer(x, g, w_in, conv_w, conv_b, w_a, b_a, w_x, b_x, lam, *, ts=256):
    b, s, d = x.shape
    ts = min(ts, s)
    bw = d // RG_BLOCKS
    return pl.pallas_call(
        functools.partial(_rec_body, ts=ts),
        grid=(b, s // ts),
        in_specs=[
            pl.BlockSpec((1, ts, d), lambda i, j: (i, j, 0)),
            _const_spec((1, d)),
            _const_spec((d, 2 * d)),
            _const_spec((CONV_W, d)),
            _const_spec((1, d)),
            _const_spec((RG_BLOCKS, bw, bw)),
            _const_spec((1, d)),
            _const_spec((RG_BLOCKS, bw, bw)),
            _const_spec((1, d)),
            _const_spec((1, d)),
        ],
        out_specs=pl.BlockSpec((1, ts, d), lambda i, j: (i, j, 0)),
        out_shape=jax.ShapeDtypeStruct((b, s, d), BF16),
        scratch_shapes=[
            pltpu.VMEM((ts + SUBLANES, d), F32),
            pltpu.VMEM((ts, d), F32),
            pltpu.VMEM((ts, d), F32),
            pltpu.VMEM((ts, d), F32),
            pltpu.VMEM((1, d), F32),
        ],
        compiler_params=pltpu.CompilerParams(
            dimension_semantics=("arbitrary", "arbitrary"), vmem_limit_bytes=VMEM_LIMIT_BYTES),
        name="rglru_mixer",
    )(x, g, w_in, conv_w, conv_b, w_a, b_a, w_x, b_x, lam)


def _attproj_body(x_ref, g_ref, w_ref, gkv_ref, wuk_ref,
                  qlat_ref, caug_ref, qi_ref, kia_ref, kib_ref, wi_ref):
    dq = H_ATT * D_QK
    di = H_IDX * D_IDX
    hn = _rms(x_ref[0], g_ref[...]).astype(BF16)
    y = _dot(hn, w_ref[...])
    ts = y.shape[0]
    o = dq
    craw = y[:, o:o + D_LAT]
    o += D_LAT
    qi_ref[0] = y[:, o:o + di].astype(BF16)
    o += di
    kia_ref[0] = y[:, o:o + LANES].astype(BF16)
    o += LANES
    kib_ref[0] = y[:, o:o + LANES].astype(BF16)
    o += LANES
    wi_ref[0] = y[:, o:o + LANES] * IDX_W_SCALE

    c = _rms(craw, gkv_ref[...])
    lane = lax.broadcasted_iota(jnp.int32, (ts, D_LAT), 1)
    ones_col = jnp.where(lane == 0, 1.0, 0.0)
    caug_ref[0] = jnp.concatenate([c, ones_col], axis=1).astype(BF16)

    for m in range(H_ATT // 2):
        qp = y[:, m * LANES:(m + 1) * LANES].astype(BF16)
        ql = _dot(qp, wuk_ref[m]) * ATT_SCALE
        qlat_ref[0, 2 * m] = ql[:, :D_LAT].astype(BF16)
        qlat_ref[0, 2 * m + 1] = ql[:, D_LAT:].astype(BF16)


def _att_proj(x, g, w_pack, g_kv, wuk2, *, ts=256):
    b, s, d = x.shape
    ts = min(ts, s)
    n_out = w_pack.shape[1]
    di = H_IDX * D_IDX
    row = lambda i, j: (i, j, 0)
    return pl.pallas_call(
        _attproj_body,
        grid=(b, s // ts),
        in_specs=[
            pl.BlockSpec((1, ts, d), row),
            _const_spec((1, d)),
            _const_spec((d, n_out)),
            _const_spec((1, D_LAT)),
            _const_spec((H_ATT // 2, 2 * D_QK, 2 * D_LAT)),
        ],
        out_specs=[
            pl.BlockSpec((1, H_ATT, ts, D_LAT), lambda i, j: (i, 0, j, 0)),
            pl.BlockSpec((1, ts, 2 * D_LAT), row),
            pl.BlockSpec((1, ts, di), row),
            pl.BlockSpec((1, ts, LANES), row),
            pl.BlockSpec((1, ts, LANES), row),
            pl.BlockSpec((1, ts, LANES), row),
        ],
        out_shape=[
            jax.ShapeDtypeStruct((b, H_ATT, s, D_LAT), BF16),
            jax.ShapeDtypeStruct((b, s, 2 * D_LAT), BF16),
            jax.ShapeDtypeStruct((b, s, di), BF16),
            jax.ShapeDtypeStruct((b, s, LANES), BF16),
            jax.ShapeDtypeStruct((b, s, LANES), BF16),
            jax.ShapeDtypeStruct((b, s, LANES), F32),
        ],
        compiler_params=pltpu.CompilerParams(
            dimension_semantics=("arbitrary", "arbitrary"), vmem_limit_bytes=VMEM_LIMIT_BYTES),
        name="att_proj",
    )(x, g, w_pack, g_kv, wuk2)


def _t5_bucket(dist):
    n = jnp.maximum(dist, 0)
    nf = jnp.maximum(n, 1).astype(F32)
    large = MAX_EXACT + (jnp.log(nf / MAX_EXACT) / math.log(MAX_DISTANCE / MAX_EXACT)
                         * (NUM_BUCKETS - MAX_EXACT)).astype(jnp.int32)
    large = jnp.minimum(large, NUM_BUCKETS - 1)
    return jnp.where(n < MAX_EXACT, n, large)


def _attn_body(rb_ref, qlat_ref, qi_ref, wi_ref, caug_ref, kia_ref, kib_ref, wuv_ref, o_ref,
               sc_std, sc_t, madd, lg, acc, mlane, btab, wib, p_s, thr, jthr, *, topk, seq):
    tq, tk = Q_TILE, K_TILE
    qb = pl.program_id(1)
    n_kt = qb // (tk // tq) + 1
    hq = H_ATT * tq

    @pl.when((pl.program_id(0) == 0) & (qb == 0))
    def _():
        ql = lax.broadcasted_iota(jnp.int32, (tq, tq), 0)
        kl = lax.broadcasted_iota(jnp.int32, (tq, tq), 1)
        for var in range(2):
            bucket = _t5_bucket(ql - kl + var * tq)
            for h in range(H_ATT):
                t = jnp.zeros((tq, tq), F32)
                for bk in range(NUM_BUCKETS - 1):
                    t = jnp.where(bucket == bk, rb_ref[bk, h] - rb_ref[NUM_BUCKETS - 1, h], t)
                btab[h, var] = t
        for h in range(H_ATT):
            btab[h, 2] = jnp.zeros((tq, tq), F32)

    qi = qi_ref[0]
    qi4 = jnp.concatenate([qi[:, m * LANES:(m + 1) * LANES] for m in range(H_IDX // 2)], axis=0)
    wi = wi_ref[0]
    for h in range(H_IDX):
        wib[h] = jnp.broadcast_to(wi[:, h:h + 1], (tq, tk))
    p_row = qb * tq + lax.broadcasted_iota(jnp.int32, (tq, tk), 0)
    lane_k = lax.broadcasted_iota(jnp.int32, (tq, tk), 1)

    def score_tile(kt, carry):
        k0 = pl.multiple_of(kt * tk, tk)
        da = jnp.maximum(_dot_nt(qi4, kia_ref[0, pl.ds(k0, tk), :]), 0.0)
        db = jnp.maximum(_dot_nt(qi4, kib_ref[0, pl.ds(k0, tk), :]), 0.0)
        s = jnp.zeros((tq, tk), F32)
        for m in range(H_IDX // 2):
            s = s + wib[2 * m] * da[m * tq:(m + 1) * tq]
            s = s + wib[2 * m + 1] * db[m * tq:(m + 1) * tq]
        s = jnp.where(lane_k + k0 <= p_row, s, NEG_INF)
        sc_std[kt] = s
        sc_t[kt] = s.T
        return carry

    lax.fori_loop(0, n_kt, score_tile, 0)

    p_lane = qb * tq + lax.broadcasted_iota(jnp.int32, (1, tq), 1)
    kq = jnp.minimum(topk, p_lane + 1).astype(F32)

    def fold(x, op):
        return op(x.reshape(tk // SUBLANES, SUBLANES, tq), axis=0)

    def count_ge(th):
        def body(kt, c):
            return c + fold(jnp.where(sc_t[kt] >= th, 1.0, 0.0), jnp.sum)
        c8 = lax.fori_loop(0, n_kt, body, jnp.zeros((SUBLANES, tq), F32))
        return jnp.sum(c8, axis=0, keepdims=True)

    def init_body(kt, c):
        mx8, mn8 = c
        t = sc_t[kt]
        mx8 = jnp.maximum(mx8, fold(t, jnp.max))
        mn8 = jnp.minimum(mn8, fold(jnp.where(t == NEG_INF, POS_INF, t), jnp.min))
        return mx8, mn8

    mx8, mn8 = lax.fori_loop(
        0, n_kt, init_body,
        (jnp.full((SUBLANES, tq), NEG_INF, F32), jnp.full((SUBLANES, tq), POS_INF, F32)))
    mx = jnp.max(mx8, axis=0, keepdims=True)
    lo0 = jnp.min(mn8, axis=0, keepdims=True)

    def bisect(_, c):
        lo, hix, chi = c
        hib = jnp.where(hix == POS_INF, mx, hix)
        mid = 0.5 * lo + 0.5 * hib
        cnt = count_ge(mid)
        ge = cnt >= kq
        return jnp.where(ge, mid, lo), jnp.where(ge, hix, mid), jnp.where(ge, chi, cnt)

    _, hix, chi = lax.fori_loop(
        0, N_BISECT, bisect,
        (lo0, jnp.full((1, tq), POS_INF, F32), jnp.zeros((1, tq), F32)))

    def snap_cond(c):
        return jnp.max(c[5]) > 0.0

    def snap_body(c):
        hix, chi, t, cge, ngt, pending = c

        def mbody(kt, m8):
            v = sc_t[kt]
            return jnp.maximum(m8, fold(jnp.where(v < hix, v, NEG_INF), jnp.max))
        tc = jnp.max(lax.fori_loop(0, n_kt, mbody, jnp.full((SUBLANES, tq), NEG_INF, F32)),
                     axis=0, keepdims=True)
        ct = count_ge(tc)
        act = pending > 0.0
        ok = ct >= kq
        t = jnp.where(act, tc, t)
        cge = jnp.where(act, ct, cge)
        ngt = jnp.where(act, chi, ngt)
        pending = jnp.where(act & ok, 0.0, pending)
        hix = jnp.where(act & ~ok, tc, hix)
        chi = jnp.where(act & ~ok, ct, chi)
        return hix, chi, t, cge, ngt, pending

    zero = jnp.zeros((1, tq), F32)
    _, _, t_k, cge, ngt, _ = lax.while_loop(
        snap_cond, snap_body, (hix, chi, zero, zero, zero, jnp.ones((1, tq), F32)))
    thr[...] = t_k
    jthr[...] = jnp.full((1, tq), float(seq), F32)

    need = kq - ngt
    excess = (cge - ngt) > need

    @pl.when(jnp.max(jnp.where(excess, 1.0, 0.0)) > 0.0)
    def _():
        sub_k = lax.broadcasted_iota(jnp.int32, (tk, tq), 0).astype(F32)

        def jbisect(_, c):
            jlo, jhi = c
            mid = jnp.floor(0.5 * (jlo + jhi))

            def body(kt, c8):
                v = sc_t[kt]
                hit = (v == t_k) & (sub_k + (kt * tk).astype(F32) <= mid)
                return c8 + fold(jnp.where(hit, 1.0, 0.0), jnp.sum)
            cnt = jnp.sum(lax.fori_loop(0, n_kt, body, jnp.zeros((SUBLANES, tq), F32)),
                          axis=0, keepdims=True)
            ge = cnt >= need
            return jnp.where(ge, jlo, mid), jnp.where(ge, mid, jhi)

        n_pass = max(1, int(math.ceil(math.log2(seq))))
        _, jhi = lax.fori_loop(
            0, n_pass, jbisect,
            (jnp.full((1, tq), -1.0, F32), jnp.full((1, tq), float(seq - 1), F32)))
        jthr[...] = jhi

    tb = jnp.broadcast_to(thr[...], (tq, tq)).T
    jb = jnp.broadcast_to(jthr[...], (tq, tq)).T
    tb2 = jnp.concatenate([tb] * (tk // tq), axis=1)
    jb2 = jnp.concatenate([jb] * (tk // tq), axis=1)
    lane_kf = lane_k.astype(F32)

    def mask_tile(kt, carry):
        s = sc_std[kt]
        jf = lane_kf + (kt * tk).astype(F32)
        sel = (s > tb2) | ((s == tb2) & (jf <= jb2))
        madd[kt] = jnp.where(sel, 0.0, NEG_INF)
        return carry

    lax.fori_loop(0, n_kt, mask_tile, 0)

    qall = qlat_ref[0].reshape(hq, D_LAT)
    for h in range(H_ATT):
        mlane[h] = jnp.full((tq, tq), NEG_INF, F32)

    def logit_tile(kt, carry):
        k0 = pl.multiple_of(kt * tk, tk)
        ct = caug_ref[0, pl.ds(k0, tk), :]
        lt = _dot_nt(qall, ct[:, :D_LAT])
        ma = madd[kt]
        d0 = jnp.clip(qb - 2 * kt, 0, 2)
        d1 = jnp.clip(qb - 2 * kt - 1, 0, 2)
        for h in range(H_ATT):
            bias = jnp.concatenate([btab[h, d0], btab[h, d1]], axis=1)
            lh = lt[h * tq:(h + 1) * tq] + ma + bias
            lg[kt, h * tq:(h + 1) * tq, :] = lh
            mlane[h] = jnp.maximum(mlane[h], jnp.maximum(lh[:, :tq], lh[:, tq:]))
        return carry

    lax.fori_loop(0, n_kt, logit_tile, 0)

    for h in range(H_ATT):
        mlane[h] = jnp.broadcast_to(jnp.max(mlane[h], axis=1, keepdims=True), (tq, tq))

    acc[...] = jnp.zeros((hq, tk), F32)

    def pv_tile(kt, carry):
        k0 = pl.multiple_of(kt * tk, tk)
        for h in range(H_ATT):
            mb = mlane[h]
            mb2 = jnp.concatenate([mb, mb], axis=1)
            p_s[h * tq:(h + 1) * tq, :] = jnp.exp(lg[kt, h * tq:(h + 1) * tq, :] - mb2).astype(BF16)
        acc[...] += _dot(p_s[...], caug_ref[0, pl.ds(k0, tk), :])
        return carry

    lax.fori_loop(0, n_kt, pv_tile, 0)

    for m in range(H_ATT // 2):
        halves = []
        for h in (2 * m, 2 * m + 1):
            blk = acc[h * tq:(h + 1) * tq, :]
            halves.append(blk[:, :D_LAT] * (1.0 / blk[:, D_LAT:D_LAT + 1]))
        pair = jnp.concatenate(halves, axis=1).astype(BF16)
        o_ref[0, :, m * 2 * D_V:(m + 1) * 2 * D_V] = _dot(pair, wuv_ref[m]).astype(BF16)


def _attention(rel_bias, qlat, qi, wi, caug, kia, kib, wuv2, *, topk):
    b, _, s, _ = qlat.shape
    tq, tk = Q_TILE, K_TILE
    assert s % tk == 0 and tk == 2 * tq
    n_kt = s // tk
    hq = H_ATT * tq
    di = H_IDX * D_IDX
    dv = H_ATT * D_V
    qrow = lambda i, j: (i, j, 0)
    full = lambda i, j: (i, 0, 0)
    return pl.pallas_call(
        functools.partial(_attn_body, topk=topk, seq=s),
        grid=(b, s // tq),
        in_specs=[
            pl.BlockSpec(memory_space=pltpu.SMEM),
            pl.BlockSpec((1, H_ATT, tq, D_LAT), lambda i, j: (i, 0, j, 0)),
            pl.BlockSpec((1, tq, di), qrow),
            pl.BlockSpec((1, tq, LANES), qrow),
            pl.BlockSpec((1, s, 2 * D_LAT), full),
            pl.BlockSpec((1, s, LANES), full),
            pl.BlockSpec((1, s, LANES), full),
            _const_spec((H_ATT // 2, 2 * D_LAT, 2 * D_V)),
        ],
        out_specs=pl.BlockSpec((1, tq, dv), qrow),
        out_shape=jax.ShapeDtypeStruct((b, s, dv), BF16),
        scratch_shapes=[
            pltpu.VMEM((n_kt, tq, tk), F32),
            pltpu.VMEM((n_kt, tk, tq), F32),
            pltpu.VMEM((n_kt, tq, tk), F32),
            pltpu.VMEM((n_kt, hq, tk), F32),
            pltpu.VMEM((hq, tk), F32),
            pltpu.VMEM((H_ATT, tq, tq), F32),
            pltpu.VMEM((H_ATT, 3, tq, tq), F32),
            pltpu.VMEM((H_IDX, tq, tk), F32),
            pltpu.VMEM((hq, tk), BF16),
            pltpu.VMEM((1, tq), F32),
            pltpu.VMEM((1, tq), F32),
        ],
        compiler_params=pltpu.CompilerParams(
            dimension_semantics=("arbitrary", "arbitrary"), vmem_limit_bytes=VMEM_LIMIT_BYTES),
        name="dsa_attention",
    )(rel_bias, qlat, qi, wi, caug, kia, kib, wuv2)


def _block_diag_pairs(w):
    h, a, b = w.shape
    w = w.reshape(h // 2, 2, a, b)
    z = jnp.zeros((h // 2, a, b), w.dtype)
    top = jnp.concatenate([w[:, 0], z], axis=2)
    bot = jnp.concatenate([z, w[:, 1]], axis=2)
    return jnp.concatenate([top, bot], axis=1)


def _pack_att_w(w_in):
    dq = H_ATT * D_QK
    di = H_IDX * D_IDX
    d = w_in.shape[0]
    o = dq + D_LAT + di
    w_ki = w_in[:, o:o + D_IDX]
    w_wi = w_in[:, o + D_IDX:o + D_IDX + H_IDX]
    z_ki = jnp.zeros((d, LANES - D_IDX), w_in.dtype)
    z_wi = jnp.zeros((d, LANES - H_IDX), w_in.dtype)
    return jnp.concatenate([w_in[:, :o], w_ki, z_ki, z_ki, w_ki, w_wi, z_wi], axis=1)


def kernel(x, norm_mix_g, norm_mlp_g, final_norm_g, rec_w_in, rec_conv_w, rec_conv_b, rec_w_a, rec_b_a, rec_w_x, rec_b_x, rec_lambda, rec_w_out, att_w_in, att_kv_norm_g, att_w_uk, att_w_uv, att_w_o, rel_bias, mlp_w_up, mlp_w_down):
    b, s, d = x.shape
    depth = norm_mix_g.shape[0]
    topk = min(TOPK_MAX, s // 4)
    row = lambda v: v.reshape(1, -1).astype(F32)
    for layer in range(depth):
        j = layer // 2
        g_mix = row(norm_mix_g[layer])
        if layer % 2 == 0:
            y = _rec_mixer(
                x, g_mix, rec_w_in[j].astype(BF16), rec_conv_w[j].astype(F32), row(rec_conv_b[j]),
                rec_w_a[j].astype(BF16), row(rec_b_a[j]), rec_w_x[j].astype(BF16), row(rec_b_x[j]),
                row(rec_lambda[j]))
            w_pre = rec_w_out[j]
        else:
            qlat, caug, qi, kia, kib, wi = _att_proj(
                x, g_mix, _pack_att_w(att_w_in[j]).astype(BF16), row(att_kv_norm_g[j]),
                _block_diag_pairs(att_w_uk[j]).astype(BF16))
            y = _attention(
                rel_bias.astype(F32), qlat, qi, wi, caug, kia, kib,
                _block_diag_pairs(att_w_uv[j]).astype(BF16), topk=topk)
            w_pre = att_w_o[j]
        x = _mlp_layer(
            x.reshape(b * s, d), y.reshape(b * s, d), w_pre.astype(BF16), row(norm_mlp_g[layer]),
            mlp_w_up[layer].astype(BF16), mlp_w_down[layer].astype(BF16), row(final_norm_g),
            final=(layer == depth - 1)).reshape(b, s, d)
    return x
```

```python
import functools
import math

import jax
import jax.numpy as jnp
from jax import lax
from jax.experimental import pallas as pl
from jax.experimental.pallas import tpu as pltpu

F32 = jnp.float32
BF16 = jnp.bfloat16

EPS = 1e-6
RG_BLOCKS = 4
CONV_W = 4
RG_C = 8.0
H_ATT = 16
D_QK = 64
D_V = 64
D_LAT = 128
H_IDX = 8
D_IDX = 64
TOPK_MAX = 256
ATT_SCALE = D_QK ** -0.5
LOG2E = math.log2(math.e)
IDX_W_SCALE = (H_IDX ** -0.5) * (D_IDX ** -0.5)
NUM_BUCKETS = 32
MAX_DISTANCE = 128
MAX_EXACT = NUM_BUCKETS // 2

LANES = 128
SUBLANES = 8
VMEM_LIMIT_BYTES = 56 * 1024 * 1024

Q_TILE = 128
K_TILE = 256
N_BISECT = 14
NEG_INF = float("-inf")
POS_INF = float("inf")


def _rms(x, g):
    ms = jnp.mean(x * x, axis=-1, keepdims=True)
    return x * lax.rsqrt(ms + EPS) * g


def _dot(a, b):
    return jnp.dot(a, b, preferred_element_type=F32)


def _dot_nt(a, b):
    return lax.dot_general(a, b, (((1,), (1,)), ((), ())), preferred_element_type=F32)


def _const_spec(shape):
    nd = len(shape)
    return pl.BlockSpec(shape, lambda *_: (0,) * nd, pipeline_mode=pl.Buffered(1))


def _mlp_body(x_ref, y_ref, wpre_ref, g_ref, wup_ref, wdn_ref, gfin_ref, o_ref, *, ff_chunk, final):
    x1 = x_ref[...] + _dot(y_ref[...], wpre_ref[...])
    hn = _rms(x1, g_ref[...]).astype(BF16)
    acc = x1
    d_ff = wup_ref.shape[1]
    for c in range(d_ff // ff_chunk):
        h = _dot(hn, wup_ref[:, c * ff_chunk:(c + 1) * ff_chunk])
        h = jnp.maximum(h, 0.0)
        h = (h * h).astype(BF16)
        acc = acc + _dot(h, wdn_ref[c * ff_chunk:(c + 1) * ff_chunk, :])
    if final:
        acc = _rms(acc, gfin_ref[...])
    o_ref[...] = acc


def _mlp_layer(x2d, y2d, w_pre, g, w_up, w_dn, g_fin, *, final, tm=512, ff_chunk=512):
    n, d = x2d.shape
    d_ff = w_up.shape[1]
    tm = min(tm, n)
    return pl.pallas_call(
        functools.partial(_mlp_body, ff_chunk=min(ff_chunk, d_ff), final=final),
        grid=(n // tm,),
        in_specs=[
            pl.BlockSpec((tm, d), lambda i: (i, 0)),
            pl.BlockSpec((tm, d), lambda i: (i, 0)),
            _const_spec((d, d)),
            _const_spec((1, d)),
            _const_spec((d, d_ff)),
            _const_spec((d_ff, d)),
            _const_spec((1, d)),
        ],
        out_specs=pl.BlockSpec((tm, d), lambda i: (i, 0)),
        out_shape=jax.ShapeDtypeStruct((n, d), F32),
        compiler_params=pltpu.CompilerParams(
            dimension_semantics=("arbitrary",), vmem_limit_bytes=VMEM_LIMIT_BYTES),
        name="mlp_block",
    )(x2d, y2d, w_pre, g, w_up, w_dn, g_fin)


def _gelu_tanh(x):
    c = math.sqrt(2.0 / math.pi)
    return 0.5 * x * (1.0 + jnp.tanh(c * (x + 0.044715 * (x * x * x))))


def _sigmoid(x):
    return 1.0 / (1.0 + jnp.exp(-x))


def _rec_body(x_ref, g_ref, win_ref, cw_ref, cb_ref, wa_ref, ba_ref, wx_ref, bx_ref, lam_ref,
              y_ref, xbuf, a_s, u_s, h_s, hcar, *, ts):
    d = x_ref.shape[-1]
    bw = d // RG_BLOCKS

    @pl.when(pl.program_id(1) == 0)
    def _():
        xbuf[0:SUBLANES, :] = jnp.zeros((SUBLANES, d), F32)
        hcar[...] = jnp.zeros((1, d), F32)

    hn = _rms(x_ref[0], g_ref[...]).astype(BF16)
    y2 = _dot(hn, win_ref[...])
    gate = _gelu_tanh(y2[:, :d])
    xr = y2[:, d:]

    xbuf[SUBLANES:SUBLANES + ts, :] = xr
    cw = cw_ref[...]
    xc = cb_ref[...] + xbuf[pl.ds(SUBLANES - 3, ts), :] * cw[0:1]
    xc = xc + xbuf[pl.ds(SUBLANES - 2, ts), :] * cw[1:2]
    xc = xc + xbuf[pl.ds(SUBLANES - 1, ts), :] * cw[2:3]
    xc = xc + xr * cw[3:4]
    xbuf[0:SUBLANES, :] = xbuf[ts:ts + SUBLANES, :]

    xcb = xc.astype(BF16)
    ra = jnp.concatenate(
        [_dot(xcb[:, n * bw:(n + 1) * bw], wa_ref[n]) for n in range(RG_BLOCKS)], axis=1)
    rx = jnp.concatenate(
        [_dot(xcb[:, n * bw:(n + 1) * bw], wx_ref[n]) for n in range(RG_BLOCKS)], axis=1)
    r = _sigmoid(ra + ba_ref[...])
    ig = _sigmoid(rx + bx_ref[...])
    nl = -lam_ref[...]
    softplus = jnp.maximum(nl, 0.0) + jnp.log1p(jnp.exp(-jnp.abs(nl)))
    log_a = (-RG_C * r) * softplus
    a = jnp.exp(log_a)
    u = jnp.sqrt(-jnp.tanh(log_a) * (a * a + 1.0)) * (ig * xc)
    a_s[...] = a
    u_s[...] = u

    row = lax.broadcasted_iota(jnp.int32, (SUBLANES, d), 0)
    masks = [(s, row >= s) for s in (1, 2, 4)]

    def chunk(c, hp):
        r0 = pl.multiple_of(c * SUBLANES, SUBLANES)
        av = a_s[pl.ds(r0, SUBLANES), :]
        uv = u_s[pl.ds(r0, SUBLANES), :]
        for s, m in masks:
            a_sh = jnp.where(m, pltpu.roll(av, s, 0), 1.0)
            u_sh = jnp.where(m, pltpu.roll(uv, s, 0), 0.0)
            uv = av * u_sh + uv
            av = av * a_sh
        hv = av * hp + uv
        h_s[pl.ds(r0, SUBLANES), :] = hv
        return hv[SUBLANES - 1:SUBLANES, :]

    hcar[...] = lax.fori_loop(0, ts // SUBLANES, chunk, hcar[...])
    y_ref[0] = (h_s[...] * gate).astype(BF16)


def _rec_mixer(x, g, w_in, conv_w, conv_b, w_a, b_a, w_x, b_x, lam, *, ts=256):
    b, s, d = x.shape
    ts = min(ts, s)
    bw = d // RG_BLOCKS
    return pl.pallas_call(
        functools.partial(_rec_body, ts=ts),
        grid=(b, s // ts),
        in_specs=[
            pl.BlockSpec((1, ts, d), lambda i, j: (i, j, 0)),
            _const_spec((1, d)),
            _const_spec((d, 2 * d)),
            _const_spec((CONV_W, d)),
            _const_spec((1, d)),
            _const_spec((RG_BLOCKS, bw, bw)),
            _const_spec((1, d)),
            _const_spec((RG_BLOCKS, bw, bw)),
            _const_spec((1, d)),
            _const_spec((1, d)),
        ],
        out_specs=pl.BlockSpec((1, ts, d), lambda i, j: (i, j, 0)),
        out_shape=jax.ShapeDtypeStruct((b, s, d), BF16),
        scratch_shapes=[
            pltpu.VMEM((ts + SUBLANES, d), F32),
            pltpu.VMEM((ts, d), F32),
            pltpu.VMEM((ts, d), F32),
            pltpu.VMEM((ts, d), F32),
            pltpu.VMEM((1, d), F32),
        ],
        compiler_params=pltpu.CompilerParams(
            dimension_semantics=("arbitrary", "arbitrary"), vmem_limit_bytes=VMEM_LIMIT_BYTES),
        name="rglru_mixer",
    )(x, g, w_in, conv_w, conv_b, w_a, b_a, w_x, b_x, lam)


def _attproj_body(x_ref, g_ref, w_ref, gkv_ref, wuk_ref,
                  qlat_ref, caug_ref, qi_ref, kia_ref, kib_ref, wi_ref):
    dq = H_ATT * D_QK
    di = H_IDX * D_IDX
    hn = _rms(x_ref[0], g_ref[...]).astype(BF16)
    y = _dot(hn, w_ref[...])
    ts = y.shape[0]
    o = dq
    craw = y[:, o:o + D_LAT]
    o += D_LAT
    qi_ref[0] = y[:, o:o + di].astype(BF16)
    o += di
    kia_ref[0] = y[:, o:o + LANES].astype(BF16)
    o += LANES
    kib_ref[0] = y[:, o:o + LANES].astype(BF16)
    o += LANES
    wi_ref[0] = y[:, o:o + LANES] * IDX_W_SCALE

    c = _rms(craw, gkv_ref[...])
    caug_ref[0] = jnp.concatenate([c, jnp.ones((ts, D_LAT), F32)], axis=1).astype(BF16)

    for m in range(H_ATT // 2):
        qp = y[:, m * LANES:(m + 1) * LANES].astype(BF16)
        ql = _dot(qp, wuk_ref[m]) * (ATT_SCALE * LOG2E)
        qlat_ref[0, 2 * m] = ql[:, :D_LAT].astype(BF16)
        qlat_ref[0, 2 * m + 1] = ql[:, D_LAT:].astype(BF16)


def _att_proj(x, g, w_pack, g_kv, wuk2, *, ts=256):
    b, s, d = x.shape
    ts = min(ts, s)
    n_out = w_pack.shape[1]
    di = H_IDX * D_IDX
    row = lambda i, j: (i, j, 0)
    return pl.pallas_call(
        _attproj_body,
        grid=(b, s // ts),
        in_specs=[
            pl.BlockSpec((1, ts, d), row),
            _const_spec((1, d)),
            _const_spec((d, n_out)),
            _const_spec((1, D_LAT)),
            _const_spec((H_ATT // 2, 2 * D_QK, 2 * D_LAT)),
        ],
        out_specs=[
            pl.BlockSpec((1, H_ATT, ts, D_LAT), lambda i, j: (i, 0, j, 0)),
            pl.BlockSpec((1, ts, 2 * D_LAT), row),
            pl.BlockSpec((1, ts, di), row),
            pl.BlockSpec((1, ts, LANES), row),
            pl.BlockSpec((1, ts, LANES), row),
            pl.BlockSpec((1, ts, LANES), row),
        ],
        out_shape=[
            jax.ShapeDtypeStruct((b, H_ATT, s, D_LAT), BF16),
            jax.ShapeDtypeStruct((b, s, 2 * D_LAT), BF16),
            jax.ShapeDtypeStruct((b, s, di), BF16),
            jax.ShapeDtypeStruct((b, s, LANES), BF16),
            jax.ShapeDtypeStruct((b, s, LANES), BF16),
            jax.ShapeDtypeStruct((b, s, LANES), F32),
        ],
        compiler_params=pltpu.CompilerParams(
            dimension_semantics=("arbitrary", "arbitrary"), vmem_limit_bytes=VMEM_LIMIT_BYTES),
        name="att_proj",
    )(x, g, w_pack, g_kv, wuk2)


def _t5_bucket(dist):
    n = jnp.maximum(dist, 0)
    nf = jnp.maximum(n, 1).astype(F32)
    large = MAX_EXACT + (jnp.log(nf / MAX_EXACT) / math.log(MAX_DISTANCE / MAX_EXACT)
                         * (NUM_BUCKETS - MAX_EXACT)).astype(jnp.int32)
    large = jnp.minimum(large, NUM_BUCKETS - 1)
    return jnp.where(n < MAX_EXACT, n, large)


def _attn_body(rb_ref, qlat_ref, qi_ref, wi_ref, caug_ref, kia_ref, kib_ref, wuv_ref, o_ref,
               sc_std, sc_t, madd, lg, acc, mlane, btab, wib, p_s, thr, jthr, *, topk, seq):
    tq, tk = Q_TILE, K_TILE
    qb = pl.program_id(1)
    n_kt = qb // (tk // tq) + 1
    hq = H_ATT * tq

    @pl.when((pl.program_id(0) == 0) & (qb == 0))
    def _():
        ql = lax.broadcasted_iota(jnp.int32, (tq, tq), 0)
        kl = lax.broadcasted_iota(jnp.int32, (tq, tq), 1)
        for var in range(2):
            bucket = _t5_bucket(ql - kl + var * tq)
            for h in range(H_ATT):
                t = jnp.zeros((tq, tq), F32)
                for bk in range(NUM_BUCKETS - 1):
                    t = jnp.where(bucket == bk, rb_ref[bk, h] - rb_ref[NUM_BUCKETS - 1, h], t)
                btab[h, var] = t * LOG2E
        for h in range(H_ATT):
            btab[h, 2] = jnp.zeros((tq, tq), F32)

    qi = qi_ref[0]
    qi4 = jnp.concatenate([qi[:, m * LANES:(m + 1) * LANES] for m in range(H_IDX // 2)], axis=0)
    wi = wi_ref[0]
    for h in range(H_IDX):
        wib[h] = jnp.broadcast_to(wi[:, h:h + 1], (tq, tk))
    p_row = qb * tq + lax.broadcasted_iota(jnp.int32, (tq, tk), 0)
    lane_k = lax.broadcasted_iota(jnp.int32, (tq, tk), 1)

    def score_tile(kt, carry):
        k0 = pl.multiple_of(kt * tk, tk)
        da = jnp.maximum(_dot_nt(qi4, kia_ref[0, pl.ds(k0, tk), :]), 0.0)
        db = jnp.maximum(_dot_nt(qi4, kib_ref[0, pl.ds(k0, tk), :]), 0.0)
        s = jnp.zeros((tq, tk), F32)
        for m in range(H_IDX // 2):
            s = s + wib[2 * m] * da[m * tq:(m + 1) * tq]
            s = s + wib[2 * m + 1] * db[m * tq:(m + 1) * tq]
        s = jnp.where(lane_k + k0 <= p_row, s, NEG_INF)
        sc_std[kt] = s
        sc_t[kt] = s.T
        return carry

    lax.fori_loop(0, n_kt, score_tile, 0)

    st = (SUBLANES, tq)
    p_lane = qb * tq + lax.broadcasted_iota(jnp.int32, st, 1)
    kq = jnp.minimum(topk, p_lane + 1).astype(F32)

    def search(n):
        def over_tiles(fn, op):
            pair = {jnp.sum: jnp.add, jnp.max: jnp.maximum, jnp.min: jnp.minimum}[op]
            parts = [op(fn(sc_t[kt].reshape(tk // SUBLANES, SUBLANES, tq), kt), axis=0)
                     for kt in range(n)]
            while len(parts) > 1:
                parts = [pair(*parts[i:i + 2]) if i + 1 < len(parts) else parts[i]
                         for i in range(0, len(parts), 2)]
            x = parts[0]
            for s in (4, 2, 1):
                x = pair(x, pltpu.roll(x, s, 0))
            return x

        def count_ge(th):
            return over_tiles(lambda v, kt: jnp.where(v >= th, 1.0, 0.0), jnp.sum)

        mx = over_tiles(lambda v, kt: v, jnp.max)
        lo0 = over_tiles(lambda v, kt: jnp.where(v == NEG_INF, POS_INF, v), jnp.min)

        def bisect(_, c):
            lo, hix, chi = c
            hib = jnp.where(hix == POS_INF, mx, hix)
            mid = 0.5 * lo + 0.5 * hib
            cnt = count_ge(mid)
            ge = cnt >= kq
            return jnp.where(ge, mid, lo), jnp.where(ge, hix, mid), jnp.where(ge, chi, cnt)

        _, hix, chi = lax.fori_loop(
            0, N_BISECT, bisect,
            (lo0, jnp.full(st, POS_INF, F32), jnp.zeros(st, F32)))

        def snap_cond(c):
            return jnp.max(c[5]) > 0.0

        def snap_body(c):
            hix, chi, t, cge, ngt, pending = c
            tc = over_tiles(lambda v, kt: jnp.where(v < hix, v, NEG_INF), jnp.max)
            ct = count_ge(tc)
            act = pending > 0.0
            ok = ct >= kq
            t = jnp.where(act, tc, t)
            cge = jnp.where(act, ct, cge)
            ngt = jnp.where(act, chi, ngt)
            pending = jnp.where(act & ok, 0.0, pending)
            hix = jnp.where(act & ~ok, tc, hix)
            chi = jnp.where(act & ~ok, ct, chi)
            return hix, chi, t, cge, ngt, pending

        zero = jnp.zeros(st, F32)
        _, _, t_k, cge, ngt, _ = lax.while_loop(
            snap_cond, snap_body, (hix, chi, zero, zero, zero, jnp.ones(st, F32)))
        thr[...] = t_k
        jthr[...] = jnp.full(st, float(seq), F32)

        need = kq - ngt
        excess = (cge - ngt) > need

        @pl.when(jnp.max(jnp.where(excess, 1.0, 0.0)) > 0.0)
        def _():
            sub_k = lax.broadcasted_iota(jnp.int32, (tk, tq), 0).astype(F32).reshape(
                tk // SUBLANES, SUBLANES, tq)

            def jbisect(_, c):
                jlo, jhi = c
                mid = jnp.floor(0.5 * (jlo + jhi))
                cnt = over_tiles(
                    lambda v, kt: jnp.where((v == t_k) & (sub_k + float(kt * tk) <= mid), 1.0, 0.0),
                    jnp.sum)
                ge = cnt >= need
                return jnp.where(ge, jlo, mid), jnp.where(ge, mid, jhi)

            n_pass = max(1, int(math.ceil(math.log2(seq))))
            _, jhi = lax.fori_loop(
                0, n_pass, jbisect,
                (jnp.full(st, -1.0, F32), jnp.full(st, float(seq - 1), F32)))
            jthr[...] = jhi

    for n in range(1, seq // tk + 1):
        pl.when(n_kt == n)(functools.partial(search, n))

    tb = jnp.concatenate([thr[...]] * (tq // SUBLANES), axis=0).T
    jb = jnp.concatenate([jthr[...]] * (tq // SUBLANES), axis=0).T
    tb2 = jnp.concatenate([tb] * (tk // tq), axis=1)
    jb2 = jnp.concatenate([jb] * (tk // tq), axis=1)
    lane_kf = lane_k.astype(F32)

    def mask_tile(kt, carry):
        s = sc_std[kt]
        jf = lane_kf + (kt * tk).astype(F32)
        sel = (s > tb2) | ((s == tb2) & (jf <= jb2))
        madd[kt] = jnp.where(sel, 0.0, NEG_INF)
        return carry

    lax.fori_loop(0, n_kt, mask_tile, 0)

    qall = qlat_ref[0].reshape(hq, D_LAT)
    for h in range(H_ATT):
        mlane[h] = jnp.full((tq, tq), NEG_INF, F32)

    def logit_tile(kt, carry, *, near):
        k0 = pl.multiple_of(kt * tk, tk)
        lt = _dot_nt(qall, caug_ref[0, pl.ds(k0, tk), :D_LAT])
        ma = madd[kt]
        if near:
            d0 = jnp.clip(qb - 2 * kt, 0, 2)
            d1 = jnp.clip(qb - 2 * kt - 1, 0, 2)
        for h in range(H_ATT):
            lh = lt[h * tq:(h + 1) * tq] + ma
            if near:
                lh = lh + jnp.concatenate([btab[h, d0], btab[h, d1]], axis=1)
            lg[kt, h * tq:(h + 1) * tq, :] = lh
            mlane[h] = jnp.maximum(mlane[h], jnp.maximum(lh[:, :tq], lh[:, tq:]))
        return carry

    n_far = jnp.maximum(qb - 1, 0) // (tk // tq)
    lax.fori_loop(0, n_far, functools.partial(logit_tile, near=False), 0)
    lax.fori_loop(n_far, n_kt, functools.partial(logit_tile, near=True), 0)

    for h in range(H_ATT):
        mlane[h] = jnp.broadcast_to(jnp.max(mlane[h], axis=1, keepdims=True), (tq, tq))

    def pv_product(kt):
        k0 = pl.multiple_of(kt * tk, tk)
        for h in range(H_ATT):
            mb = mlane[h]
            mb2 = jnp.concatenate([mb, mb], axis=1)
            p_s[h * tq:(h + 1) * tq, :] = jnp.exp2(lg[kt, h * tq:(h + 1) * tq, :] - mb2).astype(BF16)
        return _dot(p_s[...], caug_ref[0, pl.ds(k0, tk), :])

    acc[...] = pv_product(0)

    def pv_tile(kt, carry):
        acc[...] += pv_product(kt)
        return carry

    lax.fori_loop(1, n_kt, pv_tile, 0)

    for m in range(H_ATT // 2):
        halves = []
        for h in (2 * m, 2 * m + 1):
            blk = acc[h * tq:(h + 1) * tq, :]
            halves.append(blk[:, :D_LAT] * (1.0 / blk[:, D_LAT:]))
        pair = jnp.concatenate(halves, axis=1).astype(BF16)
        o_ref[0, :, m * 2 * D_V:(m + 1) * 2 * D_V] = _dot(pair, wuv_ref[m]).astype(BF16)


def _attention(rel_bias, qlat, qi, wi, caug, kia, kib, wuv2, *, topk):
    b, _, s, _ = qlat.shape
    tq, tk = Q_TILE, K_TILE
    assert s % tk == 0 and tk == 2 * tq
    n_kt = s // tk
    hq = H_ATT * tq
    di = H_IDX * D_IDX
    dv = H_ATT * D_V
    qrow = lambda i, j: (i, j, 0)
    full = lambda i, j: (i, 0, 0)
    return pl.pallas_call(
        functools.partial(_attn_body, topk=topk, seq=s),
        grid=(b, s // tq),
        in_specs=[
            pl.BlockSpec(memory_space=pltpu.SMEM),
            pl.BlockSpec((1, H_ATT, tq, D_LAT), lambda i, j: (i, 0, j, 0)),
            pl.BlockSpec((1, tq, di), qrow),
            pl.BlockSpec((1, tq, LANES), qrow),
            pl.BlockSpec((1, s, 2 * D_LAT), full),
            pl.BlockSpec((1, s, LANES), full),
            pl.BlockSpec((1, s, LANES), full),
            _const_spec((H_ATT // 2, 2 * D_LAT, 2 * D_V)),
        ],
        out_specs=pl.BlockSpec((1, tq, dv), qrow),
        out_shape=jax.ShapeDtypeStruct((b, s, dv), BF16),
        scratch_shapes=[
            pltpu.VMEM((n_kt, tq, tk), F32),
            pltpu.VMEM((n_kt, tk, tq), F32),
            pltpu.VMEM((n_kt, tq, tk), F32),
            pltpu.VMEM((n_kt, hq, tk), F32),
            pltpu.VMEM((hq, tk), F32),
            pltpu.VMEM((H_ATT, tq, tq), F32),
            pltpu.VMEM((H_ATT, 3, tq, tq), F32),
            pltpu.VMEM((H_IDX, tq, tk), F32),
            pltpu.VMEM((hq, tk), BF16),
            pltpu.VMEM((SUBLANES, tq), F32),
            pltpu.VMEM((SUBLANES, tq), F32),
        ],
        compiler_params=pltpu.CompilerParams(
            dimension_semantics=("arbitrary", "arbitrary"), vmem_limit_bytes=VMEM_LIMIT_BYTES),
        name="dsa_attention",
    )(rel_bias, qlat, qi, wi, caug, kia, kib, wuv2)


def _block_diag_pairs(w):
    h, a, b = w.shape
    w = w.reshape(h // 2, 2, a, b)
    z = jnp.zeros((h // 2, a, b), w.dtype)
    top = jnp.concatenate([w[:, 0], z], axis=2)
    bot = jnp.concatenate([z, w[:, 1]], axis=2)
    return jnp.concatenate([top, bot], axis=1)


def _pack_att_w(w_in):
    dq = H_ATT * D_QK
    di = H_IDX * D_IDX
    d = w_in.shape[0]
    o = dq + D_LAT + di
    w_ki = w_in[:, o:o + D_IDX]
    w_wi = w_in[:, o + D_IDX:o + D_IDX + H_IDX]
    z_ki = jnp.zeros((d, LANES - D_IDX), w_in.dtype)
    z_wi = jnp.zeros((d, LANES - H_IDX), w_in.dtype)
    return jnp.concatenate([w_in[:, :o], w_ki, z_ki, z_ki, w_ki, w_wi, z_wi], axis=1)


def kernel(x, norm_mix_g, norm_mlp_g, final_norm_g, rec_w_in, rec_conv_w, rec_conv_b, rec_w_a, rec_b_a, rec_w_x, rec_b_x, rec_lambda, rec_w_out, att_w_in, att_kv_norm_g, att_w_uk, att_w_uv, att_w_o, rel_bias, mlp_w_up, mlp_w_down):
    b, s, d = x.shape
    depth = norm_mix_g.shape[0]
    topk = min(TOPK_MAX, s // 4)
    row = lambda v: v.reshape(1, -1).astype(F32)
    for layer in range(depth):
        j = layer // 2
        g_mix = row(norm_mix_g[layer])
        if layer % 2 == 0:
            y = _rec_mixer(
                x, g_mix, rec_w_in[j].astype(BF16), rec_conv_w[j].astype(F32), row(rec_conv_b[j]),
                rec_w_a[j].astype(BF16), row(rec_b_a[j]), rec_w_x[j].astype(BF16), row(rec_b_x[j]),
                row(rec_lambda[j]))
            w_pre = rec_w_out[j]
        else:
            qlat, caug, qi, kia, kib, wi = _att_proj(
                x, g_mix, _pack_att_w(att_w_in[j]).astype(BF16), row(att_kv_norm_g[j]),
                _block_diag_pairs(att_w_uk[j]).astype(BF16))
            y = _attention(
                rel_bias.astype(F32), qlat, qi, wi, caug, kia, kib,
                _block_diag_pairs(att_w_uv[j]).astype(BF16), topk=topk)
            w_pre = att_w_o[j]
        x = _mlp_layer(
            x.reshape(b * s, d), y.reshape(b * s, d), w_pre.astype(BF16), row(norm_mlp_g[layer]),
            mlp_w_up[layer].astype(BF16), mlp_w_down[layer].astype(BF16), row(final_norm_g),
            final=(layer == depth - 1)).reshape(b, s, d)
    return x
```

```python
import functools
import math

import jax
import jax.numpy as jnp
from jax import lax
from jax.experimental import pallas as pl
from jax.experimental.pallas import tpu as pltpu

F32 = jnp.float32
BF16 = jnp.bfloat16

EPS = 1e-6
RG_BLOCKS = 4
CONV_W = 4
RG_C = 8.0
H_ATT = 16
D_QK = 64
D_V = 64
D_LAT = 128
H_IDX = 8
D_IDX = 64
TOPK_MAX = 256
ATT_SCALE = D_QK ** -0.5
LOG2E = math.log2(math.e)
IDX_W_SCALE = (H_IDX ** -0.5) * (D_IDX ** -0.5)
NUM_BUCKETS = 32
MAX_DISTANCE = 128
MAX_EXACT = NUM_BUCKETS // 2

LANES = 128
SUBLANES = 8
VMEM_LIMIT_BYTES = 56 * 1024 * 1024

Q_BLOCK = 128
K_TILE = 256
N_BISECT = 14
NEG_INF = float("-inf")
POS_INF = float("inf")
M_INIT = -1e30


def _rms(x, g):
    ms = jnp.mean(x * x, axis=-1, keepdims=True)
    return x * lax.rsqrt(ms + EPS) * g


def _dot(a, b):
    return jnp.dot(a, b, preferred_element_type=F32)


def _dot_nt(a, b):
    return lax.dot_general(a, b, (((1,), (1,)), ((), ())), preferred_element_type=F32)


def _const_spec(shape):
    nd = len(shape)
    return pl.BlockSpec(shape, lambda *_: (0,) * nd, pipeline_mode=pl.Buffered(1))


def _mlp_body(x_ref, y_ref, wpre_ref, g_ref, wup_ref, wdn_ref, gfin_ref, o_ref, *, ff_chunk, final):
    x1 = x_ref[...] + _dot(y_ref[...], wpre_ref[...])
    hn = _rms(x1, g_ref[...]).astype(BF16)
    acc = x1
    d_ff = wup_ref.shape[1]
    for c in range(d_ff // ff_chunk):
        h = _dot(hn, wup_ref[:, c * ff_chunk:(c + 1) * ff_chunk])
        h = jnp.maximum(h, 0.0)
        h = (h * h).astype(BF16)
        acc = acc + _dot(h, wdn_ref[c * ff_chunk:(c + 1) * ff_chunk, :])
    if final:
        acc = _rms(acc, gfin_ref[...])
    o_ref[...] = acc


def _mlp_layer(x2d, y2d, w_pre, g, w_up, w_dn, g_fin, *, final, tm=512, ff_chunk=512):
    n, d = x2d.shape
    d_ff = w_up.shape[1]
    tm = min(tm, n)
    return pl.pallas_call(
        functools.partial(_mlp_body, ff_chunk=min(ff_chunk, d_ff), final=final),
        grid=(n // tm,),
        in_specs=[
            pl.BlockSpec((tm, d), lambda i: (i, 0)),
            pl.BlockSpec((tm, d), lambda i: (i, 0)),
            _const_spec((d, d)),
            _const_spec((1, d)),
            _const_spec((d, d_ff)),
            _const_spec((d_ff, d)),
            _const_spec((1, d)),
        ],
        out_specs=pl.BlockSpec((tm, d), lambda i: (i, 0)),
        out_shape=jax.ShapeDtypeStruct((n, d), F32),
        compiler_params=pltpu.CompilerParams(
            dimension_semantics=("arbitrary",), vmem_limit_bytes=VMEM_LIMIT_BYTES),
        name="mlp_block",
    )(x2d, y2d, w_pre, g, w_up, w_dn, g_fin)


def _gelu_tanh(x):
    c = math.sqrt(2.0 / math.pi)
    return 0.5 * x * (1.0 + jnp.tanh(c * (x + 0.044715 * (x * x * x))))


def _sigmoid(x):
    return 1.0 / (1.0 + jnp.exp(-x))


def _rec_body(x_ref, g_ref, win_ref, cw_ref, cb_ref, wa_ref, ba_ref, wx_ref, bx_ref, lam_ref,
              y_ref, xbuf, a_s, u_s, h_s, hcar, *, ts):
    d = x_ref.shape[-1]
    bw = d // RG_BLOCKS

    @pl.when(pl.program_id(1) == 0)
    def _():
        xbuf[0:SUBLANES, :] = jnp.zeros((SUBLANES, d), F32)
        hcar[...] = jnp.zeros((1, d), F32)

    hn = _rms(x_ref[0], g_ref[...]).astype(BF16)
    y2 = _dot(hn, win_ref[...])
    gate = _gelu_tanh(y2[:, :d])
    xr = y2[:, d:]

    xbuf[SUBLANES:SUBLANES + ts, :] = xr
    cw = cw_ref[...]
    xc = cb_ref[...] + xbuf[pl.ds(SUBLANES - 3, ts), :] * cw[0:1]
    xc = xc + xbuf[pl.ds(SUBLANES - 2, ts), :] * cw[1:2]
    xc = xc + xbuf[pl.ds(SUBLANES - 1, ts), :] * cw[2:3]
    xc = xc + xr * cw[3:4]
    xbuf[0:SUBLANES, :] = xbuf[ts:ts + SUBLANES, :]

    xcb = xc.astype(BF16)
    ra = jnp.concatenate(
        [_dot(xcb[:, n * bw:(n + 1) * bw], wa_ref[n]) for n in range(RG_BLOCKS)], axis=1)
    rx = jnp.concatenate(
        [_dot(xcb[:, n * bw:(n + 1) * bw], wx_ref[n]) for n in range(RG_BLOCKS)], axis=1)
    r = _sigmoid(ra + ba_ref[...])
    ig = _sigmoid(rx + bx_ref[...])
    nl = -lam_ref[...]
    softplus = jnp.maximum(nl, 0.0) + jnp.log1p(jnp.exp(-jnp.abs(nl)))
    log_a = (-RG_C * r) * softplus
    a = jnp.exp(log_a)
    u = jnp.sqrt(-jnp.tanh(log_a) * (a * a + 1.0)) * (ig * xc)
    a_s[...] = a
    u_s[...] = u

    row = lax.broadcasted_iota(jnp.int32, (SUBLANES, d), 0)
    masks = [(s, row >= s) for s in (1, 2, 4)]

    def chunk(c, hp):
        r0 = pl.multiple_of(c * SUBLANES, SUBLANES)
        av = a_s[pl.ds(r0, SUBLANES), :]
        uv = u_s[pl.ds(r0, SUBLANES), :]
        for s, m in masks:
            a_sh = jnp.where(m, pltpu.roll(av, s, 0), 1.0)
            u_sh = jnp.where(m, pltpu.roll(uv, s, 0), 0.0)
            uv = av * u_sh + uv
            av = av * a_sh
        hv = av * hp + uv
        h_s[pl.ds(r0, SUBLANES), :] = hv
        return hv[SUBLANES - 1:SUBLANES, :]

    hcar[...] = lax.fori_loop(0, ts // SUBLANES, chunk, hcar[...])
    y_ref[0] = (h_s[...] * gate).astype(BF16)


def _rec_mixer(x, g, w_in, conv_w, conv_b, w_a, b_a, w_x, b_x, lam, *, ts=256):
    b, s, d = x.shape
    ts = min(ts, s)
    bw = d // RG_BLOCKS
    return pl.pallas_call(
        functools.partial(_rec_body, ts=ts),
        grid=(b, s // ts),
        in_specs=[
            pl.BlockSpec((1, ts, d), lambda i, j: (i, j, 0)),
            _const_spec((1, d)),
            _const_spec((d, 2 * d)),
            _const_spec((CONV_W, d)),
            _const_spec((1, d)),
            _const_spec((RG_BLOCKS, bw, bw)),
            _const_spec((1, d)),
            _const_spec((RG_BLOCKS, bw, bw)),
            _const_spec((1, d)),
            _const_spec((1, d)),
        ],
        out_specs=pl.BlockSpec((1, ts, d), lambda i, j: (i, j, 0)),
        out_shape=jax.ShapeDtypeStruct((b, s, d), BF16),
        scratch_shapes=[
            pltpu.VMEM((ts + SUBLANES, d), F32),
            pltpu.VMEM((ts, d), F32),
            pltpu.VMEM((ts, d), F32),
            pltpu.VMEM((ts, d), F32),
            pltpu.VMEM((1, d), F32),
        ],
        compiler_params=pltpu.CompilerParams(
            dimension_semantics=("arbitrary", "arbitrary"), vmem_limit_bytes=VMEM_LIMIT_BYTES),
        name="rglru_mixer",
    )(x, g, w_in, conv_w, conv_b, w_a, b_a, w_x, b_x, lam)


def _attproj_body(x_ref, g_ref, w_ref, gkv_ref, wuk_ref,
                  qlat_ref, caug_ref, qi_ref, kia_ref, kib_ref, wi_ref):
    dq = H_ATT * D_QK
    di = H_IDX * D_IDX
    hn = _rms(x_ref[0], g_ref[...]).astype(BF16)
    y = _dot(hn, w_ref[...])
    ts = y.shape[0]
    o = dq
    craw = y[:, o:o + D_LAT]
    o += D_LAT
    qi_ref[0] = y[:, o:o + di].astype(BF16)
    o += di
    kia_ref[0] = y[:, o:o + LANES].astype(BF16)
    o += LANES
    kib_ref[0] = y[:, o:o + LANES].astype(BF16)
    o += LANES
    wi_ref[0] = y[:, o:o + LANES] * IDX_W_SCALE

    c = _rms(craw, gkv_ref[...])
    caug_ref[0] = jnp.concatenate([c, jnp.ones((ts, D_LAT), F32)], axis=1).astype(BF16)

    for m in range(H_ATT // 2):
        qp = y[:, m * LANES:(m + 1) * LANES].astype(BF16)
        ql = _dot(qp, wuk_ref[m]) * (ATT_SCALE * LOG2E)
        qlat_ref[0, 2 * m] = ql[:, :D_LAT].astype(BF16)
        qlat_ref[0, 2 * m + 1] = ql[:, D_LAT:].astype(BF16)


def _att_proj(x, g, w_pack, g_kv, wuk2, *, ts=256):
    b, s, d = x.shape
    ts = min(ts, s)
    n_out = w_pack.shape[1]
    di = H_IDX * D_IDX
    row = lambda i, j: (i, j, 0)
    return pl.pallas_call(
        _attproj_body,
        grid=(b, s // ts),
        in_specs=[
            pl.BlockSpec((1, ts, d), row),
            _const_spec((1, d)),
            _const_spec((d, n_out)),
            _const_spec((1, D_LAT)),
            _const_spec((H_ATT // 2, 2 * D_QK, 2 * D_LAT)),
        ],
        out_specs=[
            pl.BlockSpec((1, H_ATT, ts, D_LAT), lambda i, j: (i, 0, j, 0)),
            pl.BlockSpec((1, ts, 2 * D_LAT), row),
            pl.BlockSpec((1, ts, di), row),
            pl.BlockSpec((1, ts, LANES), row),
            pl.BlockSpec((1, ts, LANES), row),
            pl.BlockSpec((1, ts, LANES), row),
        ],
        out_shape=[
            jax.ShapeDtypeStruct((b, H_ATT, s, D_LAT), BF16),
            jax.ShapeDtypeStruct((b, s, 2 * D_LAT), BF16),
            jax.ShapeDtypeStruct((b, s, di), BF16),
            jax.ShapeDtypeStruct((b, s, LANES), BF16),
            jax.ShapeDtypeStruct((b, s, LANES), BF16),
            jax.ShapeDtypeStruct((b, s, LANES), F32),
        ],
        compiler_params=pltpu.CompilerParams(
            dimension_semantics=("arbitrary", "arbitrary"), vmem_limit_bytes=VMEM_LIMIT_BYTES),
        name="att_proj",
    )(x, g, w_pack, g_kv, wuk2)


def _t5_bucket(dist):
    n = jnp.maximum(dist, 0)
    nf = jnp.maximum(n, 1).astype(F32)
    large = MAX_EXACT + (jnp.log(nf / MAX_EXACT) / math.log(MAX_DISTANCE / MAX_EXACT)
                         * (NUM_BUCKETS - MAX_EXACT)).astype(jnp.int32)
    large = jnp.minimum(large, NUM_BUCKETS - 1)
    return jnp.where(n < MAX_EXACT, n, large)


def _attn_body(rb_ref, qlat_ref, qi_ref, wi_ref, caug_ref, kia_ref, kib_ref, wuv_ref, o_ref,
               sc_std, sc_t, acc, mrun, alph, btab, wib, p_s, tj, thr, jthr, *, topk, seq):
    tq, tk = Q_BLOCK, K_TILE
    nb = tk // tq
    step = pl.program_id(1)
    n_kt = step + 1
    hq = H_ATT * tq
    pairs = H_IDX // 2

    @pl.when((pl.program_id(0) == 0) & (step == 0))
    def _():
        ql = lax.broadcasted_iota(jnp.int32, (tq, tq), 0)
        kl = lax.broadcasted_iota(jnp.int32, (tq, tq), 1)
        for var in range(2):
            bucket = _t5_bucket(ql - kl + var * tq)
            for h in range(H_ATT):
                t = jnp.zeros((tq, tq), F32)
                for bk in range(NUM_BUCKETS - 1):
                    t = jnp.where(bucket == bk, rb_ref[bk, h] - rb_ref[NUM_BUCKETS - 1, h], t)
                btab[h, var] = t * LOG2E

    qi = qi_ref[0]
    qi8 = jnp.concatenate(
        [qi[blk * tq:(blk + 1) * tq, m * LANES:(m + 1) * LANES]
         for blk in range(nb) for m in range(pairs)], axis=0)
    wi = wi_ref[0]
    for blk in range(nb):
        for h in range(H_IDX):
            wib[blk, h] = jnp.broadcast_to(wi[blk * tq:(blk + 1) * tq, h:h + 1], (tq, tk))
    row_q = lax.broadcasted_iota(jnp.int32, (tq, tk), 0)
    lane_k = lax.broadcasted_iota(jnp.int32, (tq, tk), 1)

    def score_tile(kt, carry, *, last):
        k0 = pl.multiple_of(kt * tk, tk)
        da = jnp.maximum(_dot_nt(qi8, kia_ref[0, pl.ds(k0, tk), :]), 0.0)
        db = jnp.maximum(_dot_nt(qi8, kib_ref[0, pl.ds(k0, tk), :]), 0.0)
        for blk in range(nb):
            s = jnp.zeros((tq, tk), F32)
            for m in range(pairs):
                r0 = (blk * pairs + m) * tq
                s = s + wib[blk, 2 * m] * da[r0:r0 + tq]
                s = s + wib[blk, 2 * m + 1] * db[r0:r0 + tq]
            if last:
                s = jnp.where(lane_k <= row_q + blk * tq, s, NEG_INF)
            sc_std[blk, kt] = s
            sc_t[kt, :, blk * tq:(blk + 1) * tq] = s.T
        return carry

    lax.fori_loop(0, step, functools.partial(score_tile, last=False), 0)
    score_tile(step, 0, last=True)

    st = (SUBLANES, nb * tq)
    p_lane = step * (nb * tq) + lax.broadcasted_iota(jnp.int32, st, 1)
    kq = jnp.minimum(topk, p_lane + 1).astype(F32)

    def search(n):
        def over_tiles(fn, op):
            pair = {jnp.sum: jnp.add, jnp.max: jnp.maximum, jnp.min: jnp.minimum}[op]
            parts = [op(fn(sc_t[kt].reshape(tk // SUBLANES, SUBLANES, nb * tq), kt), axis=0)
                     for kt in range(n)]
            while len(parts) > 1:
                parts = [pair(*parts[i:i + 2]) if i + 1 < len(parts) else parts[i]
                         for i in range(0, len(parts), 2)]
            x = parts[0]
            for s in (4, 2, 1):
                x = pair(x, pltpu.roll(x, s, 0))
            return x

        def count_ge(th):
            return over_tiles(lambda v, kt: jnp.where(v >= th, 1.0, 0.0), jnp.sum)

        mx = over_tiles(lambda v, kt: v, jnp.max)
        lo0 = over_tiles(lambda v, kt: jnp.where(v == NEG_INF, POS_INF, v), jnp.min)

        def bisect(_, c):
            lo, hix, chi = c
            hib = jnp.where(hix == POS_INF, mx, hix)
            mid = 0.5 * lo + 0.5 * hib
            cnt = count_ge(mid)
            ge = cnt >= kq
            return jnp.where(ge, mid, lo), jnp.where(ge, hix, mid), jnp.where(ge, chi, cnt)

        _, hix, chi = lax.fori_loop(
            0, N_BISECT, bisect,
            (lo0, jnp.full(st, POS_INF, F32), jnp.zeros(st, F32)))

        def snap_cond(c):
            return jnp.max(c[5]) > 0.0

        def snap_body(c):
            hix, chi, t, cge, ngt, pending = c
            tc = over_tiles(lambda v, kt: jnp.where(v < hix, v, NEG_INF), jnp.max)
            ct = count_ge(tc)
            act = pending > 0.0
            ok = ct >= kq
            t = jnp.where(act, tc, t)
            cge = jnp.where(act, ct, cge)
            ngt = jnp.where(act, chi, ngt)
            pending = jnp.where(act & ok, 0.0, pending)
            hix = jnp.where(act & ~ok, tc, hix)
            chi = jnp.where(act & ~ok, ct, chi)
            return hix, chi, t, cge, ngt, pending

        zero = jnp.zeros(st, F32)
        _, _, t_k, cge, ngt, _ = lax.while_loop(
            snap_cond, snap_body, (hix, chi, zero, zero, zero, jnp.ones(st, F32)))
        thr[...] = t_k
        jthr[...] = jnp.full(st, float(seq), F32)

        need = kq - ngt
        excess = (cge - ngt) > need

        @pl.when(jnp.max(jnp.where(excess, 1.0, 0.0)) > 0.0)
        def _():
            sub_k = lax.broadcasted_iota(jnp.int32, (tk, nb * tq), 0).astype(F32).reshape(
                tk // SUBLANES, SUBLANES, nb * tq)

            def jbisect(_, c):
                jlo, jhi = c
                mid = jnp.floor(0.5 * (jlo + jhi))
                cnt = over_tiles(
                    lambda v, kt: jnp.where((v == t_k) & (sub_k + float(kt * tk) <= mid), 1.0, 0.0),
                    jnp.sum)
                ge = cnt >= need
                return jnp.where(ge, jlo, mid), jnp.where(ge, mid, jhi)

            n_pass = max(1, int(math.ceil(math.log2(seq))))
            _, jhi = lax.fori_loop(
                0, n_pass, jbisect,
                (jnp.full(st, -1.0, F32), jnp.full(st, float(seq - 1), F32)))
            jthr[...] = jhi

    for n in range(1, seq // tk + 1):
        pl.when(n_kt == n)(functools.partial(search, n))

    for blk in range(nb):
        for i, ref in enumerate((thr, jthr)):
            v = jnp.concatenate([ref[:, blk * tq:(blk + 1) * tq]] * (tq // SUBLANES), axis=0).T
            tj[blk, i] = jnp.concatenate([v] * (tk // tq), axis=1)

    qall = jnp.concatenate(
        [qlat_ref[0, :, blk * tq:(blk + 1) * tq, :].reshape(hq, D_LAT) for blk in range(nb)],
        axis=0)
    for blk in range(nb):
        for h in range(H_ATT):
            mrun[blk, h] = jnp.full((tq, tq), M_INIT, F32)
    acc[...] = jnp.zeros(acc.shape, F32)
    lane_kf = lane_k.astype(F32)

    def attend_tile(kt, carry, *, kind):
        k0 = pl.multiple_of(kt * tk, tk)
        ct = caug_ref[0, pl.ds(k0, tk), :]
        lt = _dot_nt(qall, ct[:, :D_LAT])
        kf = lane_kf + (kt * tk).astype(F32)
        for blk in range(nb):
            s = sc_std[blk, kt]
            tb = tj[blk, 0]
            sel = (s > tb) | ((s == tb) & (kf <= tj[blk, 1]))
            ma = jnp.where(sel, 0.0, NEG_INF)
            for h in range(H_ATT):
                r0 = (blk * H_ATT + h) * tq
                lh = lt[r0:r0 + tq] + ma
                if kind == "near" and blk == 0:
                    lh = lh + jnp.concatenate([jnp.zeros((tq, tq), F32), btab[h, 1]], axis=1)
                elif kind == "last" and blk == 0:
                    lh = lh + jnp.concatenate([btab[h, 0], jnp.zeros((tq, tq), F32)], axis=1)
                elif kind == "last" and blk == 1:
                    lh = lh + jnp.concatenate([btab[h, 1], btab[h, 0]], axis=1)
                mt = jnp.maximum(lh[:, :tq], lh[:, tq:])
                rm = jnp.broadcast_to(jnp.max(mt, axis=1, keepdims=True), (tq, tq))
                m_old = mrun[blk, h]
                m_new = jnp.maximum(m_old, rm)
                mrun[blk, h] = m_new
                alph[blk, h] = jnp.exp2(m_old - m_new)
                m2 = jnp.concatenate([m_new, m_new], axis=1)
                p_s[r0:r0 + tq, :] = jnp.exp2(lh - m2).astype(BF16)
        pv = _dot(p_s[...], ct)
        for blk in range(nb):
            for h in range(H_ATT):
                r0 = (blk * H_ATT + h) * tq
                al = alph[blk, h]
                acc[r0:r0 + tq, :] = acc[r0:r0 + tq, :] * jnp.concatenate([al, al], axis=1) + pv[r0:r0 + tq]
        return carry

    lax.fori_loop(0, jnp.maximum(step - 1, 0), functools.partial(attend_tile, kind="far"), 0)
    @pl.when(step >= 1)
    def _():
        attend_tile(step - 1, 0, kind="near")

    attend_tile(step, 0, kind="last")

    for m in range(H_ATT // 2):
        rows = []
        for blk in range(nb):
            halves = []
            for h in (2 * m, 2 * m + 1):
                r0 = (blk * H_ATT + h) * tq
                a = acc[r0:r0 + tq, :]
                halves.append(a[:, :D_LAT] * (1.0 / a[:, D_LAT:]))
            rows.append(jnp.concatenate(halves, axis=1))
        pair = jnp.concatenate(rows, axis=0).astype(BF16)
        o_ref[0, :, m * 2 * D_V:(m + 1) * 2 * D_V] = _dot(pair, wuv_ref[m]).astype(BF16)


def _attention(rel_bias, qlat, qi, wi, caug, kia, kib, wuv2, *, topk):
    b, _, s, _ = qlat.shape
    tq, tk = Q_BLOCK, K_TILE
    assert s % tk == 0 and tk == 2 * tq
    nb = tk // tq
    n_kt = s // tk
    hq = H_ATT * tq
    di = H_IDX * D_IDX
    dv = H_ATT * D_V
    qrow = lambda i, j: (i, j, 0)
    full = lambda i, j: (i, 0, 0)
    return pl.pallas_call(
        functools.partial(_attn_body, topk=topk, seq=s),
        grid=(b, s // tk),
        in_specs=[
            pl.BlockSpec(memory_space=pltpu.SMEM),
            pl.BlockSpec((1, H_ATT, tk, D_LAT), lambda i, j: (i, 0, j, 0)),
            pl.BlockSpec((1, tk, di), qrow),
            pl.BlockSpec((1, tk, LANES), qrow),
            pl.BlockSpec((1, s, 2 * D_LAT), full),
            pl.BlockSpec((1, s, LANES), full),
            pl.BlockSpec((1, s, LANES), full),
            _const_spec((H_ATT // 2, 2 * D_LAT, 2 * D_V)),
        ],
        out_specs=pl.BlockSpec((1, tk, dv), qrow),
        out_shape=jax.ShapeDtypeStruct((b, s, dv), BF16),
        scratch_shapes=[
            pltpu.VMEM((nb, n_kt, tq, tk), F32),
            pltpu.VMEM((n_kt, tk, nb * tq), F32),
            pltpu.VMEM((nb * hq, tk), F32),
            pltpu.VMEM((nb, H_ATT, tq, tq), F32),
            pltpu.VMEM((nb, H_ATT, tq, tq), F32),
            pltpu.VMEM((H_ATT, 2, tq, tq), F32),
            pltpu.VMEM((nb, H_IDX, tq, tk), F32),
            pltpu.VMEM((nb * hq, tk), BF16),
            pltpu.VMEM((nb, 2, tq, tk), F32),
            pltpu.VMEM((SUBLANES, nb * tq), F32),
            pltpu.VMEM((SUBLANES, nb * tq), F32),
        ],
        compiler_params=pltpu.CompilerParams(
            dimension_semantics=("arbitrary", "arbitrary"), vmem_limit_bytes=VMEM_LIMIT_BYTES),
        name="dsa_attention",
    )(rel_bias, qlat, qi, wi, caug, kia, kib, wuv2)


def _block_diag_pairs(w):
    h, a, b = w.shape
    w = w.reshape(h // 2, 2, a, b)
    z = jnp.zeros((h // 2, a, b), w.dtype)
    top = jnp.concatenate([w[:, 0], z], axis=2)
    bot = jnp.concatenate([z, w[:, 1]], axis=2)
    return jnp.concatenate([top, bot], axis=1)


def _pack_att_w(w_in):
    dq = H_ATT * D_QK
    di = H_IDX * D_IDX
    d = w_in.shape[0]
    o = dq + D_LAT + di
    w_ki = w_in[:, o:o + D_IDX]
    w_wi = w_in[:, o + D_IDX:o + D_IDX + H_IDX]
    z_ki = jnp.zeros((d, LANES - D_IDX), w_in.dtype)
    z_wi = jnp.zeros((d, LANES - H_IDX), w_in.dtype)
    return jnp.concatenate([w_in[:, :o], w_ki, z_ki, z_ki, w_ki, w_wi, z_wi], axis=1)


def kernel(x, norm_mix_g, norm_mlp_g, final_norm_g, rec_w_in, rec_conv_w, rec_conv_b, rec_w_a, rec_b_a, rec_w_x, rec_b_x, rec_lambda, rec_w_out, att_w_in, att_kv_norm_g, att_w_uk, att_w_uv, att_w_o, rel_bias, mlp_w_up, mlp_w_down):
    b, s, d = x.shape
    depth = norm_mix_g.shape[0]
    topk = min(TOPK_MAX, s // 4)
    row = lambda v: v.reshape(1, -1).astype(F32)
    for layer in range(depth):
        j = layer // 2
        g_mix = row(norm_mix_g[layer])
        if layer % 2 == 0:
            y = _rec_mixer(
                x, g_mix, rec_w_in[j].astype(BF16), rec_conv_w[j].astype(F32), row(rec_conv_b[j]),
                rec_w_a[j].astype(BF16), row(rec_b_a[j]), rec_w_x[j].astype(BF16), row(rec_b_x[j]),
                row(rec_lambda[j]))
            w_pre = rec_w_out[j]
        else:
            qlat, caug, qi, kia, kib, wi = _att_proj(
                x, g_mix, _pack_att_w(att_w_in[j]).astype(BF16), row(att_kv_norm_g[j]),
                _block_diag_pairs(att_w_uk[j]).astype(BF16))
            y = _attention(
                rel_bias.astype(F32), qlat, qi, wi, caug, kia, kib,
                _block_diag_pairs(att_w_uv[j]).astype(BF16), topk=topk)
            w_pre = att_w_o[j]
        x = _mlp_layer(
            x.reshape(b * s, d), y.reshape(b * s, d), w_pre.astype(BF16), row(norm_mlp_g[layer]),
            mlp_w_up[layer].astype(BF16), mlp_w_down[layer].astype(BF16), row(final_norm_g),
            final=(layer == depth - 1)).reshape(b, s, d)
    return x
```

```python
import functools
import math

import jax
import jax.numpy as jnp
from jax import lax
from jax.experimental import pallas as pl
from jax.experimental.pallas import tpu as pltpu

F32 = jnp.float32
BF16 = jnp.bfloat16

EPS = 1e-6
RG_BLOCKS = 4
CONV_W = 4
RG_C = 8.0
H_ATT = 16
D_QK = 64
D_V = 64
D_LAT = 128
H_IDX = 8
D_IDX = 64
TOPK_MAX = 256
ATT_SCALE = D_QK ** -0.5
LOG2E = math.log2(math.e)
IDX_W_SCALE = (H_IDX ** -0.5) * (D_IDX ** -0.5)
NUM_BUCKETS = 32
MAX_DISTANCE = 128
MAX_EXACT = NUM_BUCKETS // 2

LANES = 128
SUBLANES = 8
VMEM_LIMIT_BYTES = 56 * 1024 * 1024

Q_BLOCK = 128
K_TILE = 256
N_BISECT = 14
NEG_INF = float("-inf")
POS_INF = float("inf")
M_INIT = -1e30


def _rms(x, g):
    ms = jnp.mean(x * x, axis=-1, keepdims=True)
    return x * lax.rsqrt(ms + EPS) * g


def _dot(a, b):
    return jnp.dot(a, b, preferred_element_type=F32)


def _dot_nt(a, b):
    return lax.dot_general(a, b, (((1,), (1,)), ((), ())), preferred_element_type=F32)


def _const_spec(shape):
    nd = len(shape)
    return pl.BlockSpec(shape, lambda *_: (0,) * nd, pipeline_mode=pl.Buffered(1))


def _mlp_body(x_ref, y_ref, wpre_ref, g_ref, wup_ref, wdn_ref, gfin_ref, o_ref, *, ff_chunk, final):
    x1 = x_ref[...] + _dot(y_ref[...], wpre_ref[...])
    hn = _rms(x1, g_ref[...]).astype(BF16)
    acc = x1
    d_ff = wup_ref.shape[1]
    for c in range(d_ff // ff_chunk):
        h = _dot(hn, wup_ref[:, c * ff_chunk:(c + 1) * ff_chunk])
        h = jnp.maximum(h, 0.0)
        h = (h * h).astype(BF16)
        acc = acc + _dot(h, wdn_ref[c * ff_chunk:(c + 1) * ff_chunk, :])
    if final:
        acc = _rms(acc, gfin_ref[...])
    o_ref[...] = acc


def _mlp_layer(x2d, y2d, w_pre, g, w_up, w_dn, g_fin, *, final, tm=512, ff_chunk=512):
    n, d = x2d.shape
    d_ff = w_up.shape[1]
    tm = min(tm, n)
    return pl.pallas_call(
        functools.partial(_mlp_body, ff_chunk=min(ff_chunk, d_ff), final=final),
        grid=(n // tm,),
        in_specs=[
            pl.BlockSpec((tm, d), lambda i: (i, 0)),
            pl.BlockSpec((tm, d), lambda i: (i, 0)),
            _const_spec((d, d)),
            _const_spec((1, d)),
            _const_spec((d, d_ff)),
            _const_spec((d_ff, d)),
            _const_spec((1, d)),
        ],
        out_specs=pl.BlockSpec((tm, d), lambda i: (i, 0)),
        out_shape=jax.ShapeDtypeStruct((n, d), F32),
        compiler_params=pltpu.CompilerParams(
            dimension_semantics=("arbitrary",), vmem_limit_bytes=VMEM_LIMIT_BYTES),
        name="mlp_block",
    )(x2d, y2d, w_pre, g, w_up, w_dn, g_fin)


def _gelu_tanh(x):
    c = math.sqrt(2.0 / math.pi)
    return 0.5 * x * (1.0 + jnp.tanh(c * (x + 0.044715 * (x * x * x))))


def _sigmoid(x):
    return 1.0 / (1.0 + jnp.exp(-x))


def _rec_body(x_ref, g_ref, win_ref, cw_ref, cb_ref, wa_ref, ba_ref, wx_ref, bx_ref, lam_ref,
              y_ref, xbuf, a_s, u_s, h_s, hcar, *, ts):
    d = x_ref.shape[-1]
    bw = d // RG_BLOCKS

    @pl.when(pl.program_id(1) == 0)
    def _():
        xbuf[0:SUBLANES, :] = jnp.zeros((SUBLANES, d), F32)
        hcar[...] = jnp.zeros((1, d), F32)

    hn = _rms(x_ref[0], g_ref[...]).astype(BF16)
    y2 = _dot(hn, win_ref[...])
    gate = _gelu_tanh(y2[:, :d])
    xr = y2[:, d:]

    xbuf[SUBLANES:SUBLANES + ts, :] = xr
    cw = cw_ref[...]
    xc = cb_ref[...] + xbuf[pl.ds(SUBLANES - 3, ts), :] * cw[0:1]
    xc = xc + xbuf[pl.ds(SUBLANES - 2, ts), :] * cw[1:2]
    xc = xc + xbuf[pl.ds(SUBLANES - 1, ts), :] * cw[2:3]
    xc = xc + xr * cw[3:4]
    xbuf[0:SUBLANES, :] = xbuf[ts:ts + SUBLANES, :]

    xcb = xc.astype(BF16)
    ra = jnp.concatenate(
        [_dot(xcb[:, n * bw:(n + 1) * bw], wa_ref[n]) for n in range(RG_BLOCKS)], axis=1)
    rx = jnp.concatenate(
        [_dot(xcb[:, n * bw:(n + 1) * bw], wx_ref[n]) for n in range(RG_BLOCKS)], axis=1)
    r = _sigmoid(ra + ba_ref[...])
    ig = _sigmoid(rx + bx_ref[...])
    nl = -lam_ref[...]
    softplus = jnp.maximum(nl, 0.0) + jnp.log1p(jnp.exp(-jnp.abs(nl)))
    log_a = (-RG_C * r) * softplus
    a = jnp.exp(log_a)
    v = -jnp.tanh(log_a) * (a * a + 1.0)
    root = jnp.where(v > 0.0, v * lax.rsqrt(v), 0.0)
    u = root * (ig * xc)
    a_s[...] = a
    u_s[...] = u

    row = lax.broadcasted_iota(jnp.int32, (SUBLANES, d), 0)
    masks = [(s, row >= s) for s in (1, 2, 4)]

    def chunk(c, hp):
        r0 = pl.multiple_of(c * SUBLANES, SUBLANES)
        av = a_s[pl.ds(r0, SUBLANES), :]
        uv = u_s[pl.ds(r0, SUBLANES), :]
        for s, m in masks:
            a_sh = jnp.where(m, pltpu.roll(av, s, 0), 1.0)
            u_sh = jnp.where(m, pltpu.roll(uv, s, 0), 0.0)
            uv = av * u_sh + uv
            av = av * a_sh
        hv = av * hp + uv
        h_s[pl.ds(r0, SUBLANES), :] = hv
        return hv[SUBLANES - 1:SUBLANES, :]

    hcar[...] = lax.fori_loop(0, ts // SUBLANES, chunk, hcar[...])
    y_ref[0] = (h_s[...] * gate).astype(BF16)


def _rec_mixer(x, g, w_in, conv_w, conv_b, w_a, b_a, w_x, b_x, lam, *, ts=256):
    b, s, d = x.shape
    ts = min(ts, s)
    bw = d // RG_BLOCKS
    return pl.pallas_call(
        functools.partial(_rec_body, ts=ts),
        grid=(b, s // ts),
        in_specs=[
            pl.BlockSpec((1, ts, d), lambda i, j: (i, j, 0)),
            _const_spec((1, d)),
            _const_spec((d, 2 * d)),
            _const_spec((CONV_W, d)),
            _const_spec((1, d)),
            _const_spec((RG_BLOCKS, bw, bw)),
            _const_spec((1, d)),
            _const_spec((RG_BLOCKS, bw, bw)),
            _const_spec((1, d)),
            _const_spec((1, d)),
        ],
        out_specs=pl.BlockSpec((1, ts, d), lambda i, j: (i, j, 0)),
        out_shape=jax.ShapeDtypeStruct((b, s, d), BF16),
        scratch_shapes=[
            pltpu.VMEM((ts + SUBLANES, d), F32),
            pltpu.VMEM((ts, d), F32),
            pltpu.VMEM((ts, d), F32),
            pltpu.VMEM((ts, d), F32),
            pltpu.VMEM((1, d), F32),
        ],
        compiler_params=pltpu.CompilerParams(
            dimension_semantics=("arbitrary", "arbitrary"), vmem_limit_bytes=VMEM_LIMIT_BYTES),
        name="rglru_mixer",
    )(x, g, w_in, conv_w, conv_b, w_a, b_a, w_x, b_x, lam)


def _attproj_body(x_ref, g_ref, w_ref, gkv_ref, wuk_ref,
                  qlat_ref, caug_ref, qi_ref, kia_ref, kib_ref, wi_ref):
    dq = H_ATT * D_QK
    di = H_IDX * D_IDX
    hn = _rms(x_ref[0], g_ref[...]).astype(BF16)
    y = _dot(hn, w_ref[...])
    ts = y.shape[0]
    o = dq
    craw = y[:, o:o + D_LAT]
    o += D_LAT
    qi_ref[0] = y[:, o:o + di].astype(BF16)
    o += di
    kia_ref[0] = y[:, o:o + LANES].astype(BF16)
    o += LANES
    kib_ref[0] = y[:, o:o + LANES].astype(BF16)
    o += LANES
    wi_ref[0] = y[:, o:o + LANES] * IDX_W_SCALE

    c = _rms(craw, gkv_ref[...])
    caug_ref[0] = jnp.concatenate([c, jnp.ones((ts, D_LAT), F32)], axis=1).astype(BF16)

    for m in range(H_ATT // 2):
        qp = y[:, m * LANES:(m + 1) * LANES].astype(BF16)
        ql = _dot(qp, wuk_ref[m]) * (ATT_SCALE * LOG2E)
        qlat_ref[0, 2 * m] = ql[:, :D_LAT].astype(BF16)
        qlat_ref[0, 2 * m + 1] = ql[:, D_LAT:].astype(BF16)


def _att_proj(x, g, w_pack, g_kv, wuk2, *, ts=256):
    b, s, d = x.shape
    ts = min(ts, s)
    n_out = w_pack.shape[1]
    di = H_IDX * D_IDX
    row = lambda i, j: (i, j, 0)
    return pl.pallas_call(
        _attproj_body,
        grid=(b, s // ts),
        in_specs=[
            pl.BlockSpec((1, ts, d), row),
            _const_spec((1, d)),
            _const_spec((d, n_out)),
            _const_spec((1, D_LAT)),
            _const_spec((H_ATT // 2, 2 * D_QK, 2 * D_LAT)),
        ],
        out_specs=[
            pl.BlockSpec((1, H_ATT, ts, D_LAT), lambda i, j: (i, 0, j, 0)),
            pl.BlockSpec((1, ts, 2 * D_LAT), row),
            pl.BlockSpec((1, ts, di), row),
            pl.BlockSpec((1, ts, LANES), row),
            pl.BlockSpec((1, ts, LANES), row),
            pl.BlockSpec((1, ts, LANES), row),
        ],
        out_shape=[
            jax.ShapeDtypeStruct((b, H_ATT, s, D_LAT), BF16),
            jax.ShapeDtypeStruct((b, s, 2 * D_LAT), BF16),
            jax.ShapeDtypeStruct((b, s, di), BF16),
            jax.ShapeDtypeStruct((b, s, LANES), BF16),
            jax.ShapeDtypeStruct((b, s, LANES), BF16),
            jax.ShapeDtypeStruct((b, s, LANES), F32),
        ],
        compiler_params=pltpu.CompilerParams(
            dimension_semantics=("arbitrary", "arbitrary"), vmem_limit_bytes=VMEM_LIMIT_BYTES),
        name="att_proj",
    )(x, g, w_pack, g_kv, wuk2)


def _t5_bucket(dist):
    n = jnp.maximum(dist, 0)
    nf = jnp.maximum(n, 1).astype(F32)
    large = MAX_EXACT + (jnp.log(nf / MAX_EXACT) / math.log(MAX_DISTANCE / MAX_EXACT)
                         * (NUM_BUCKETS - MAX_EXACT)).astype(jnp.int32)
    large = jnp.minimum(large, NUM_BUCKETS - 1)
    return jnp.where(n < MAX_EXACT, n, large)


def _attn_body(rb_ref, qlat_ref, qi_ref, wi_ref, caug_ref, kia_ref, kib_ref, wuv_ref, o_ref,
               sc_std, sc_t, acc, mrun, alph, btab, wib, p_s, tj, thr, jthr, *, topk, seq):
    tq, tk = Q_BLOCK, K_TILE
    nb = tk // tq
    step = pl.program_id(1)
    n_kt = step + 1
    hq = H_ATT * tq
    pairs = H_IDX // 2

    @pl.when((pl.program_id(0) == 0) & (step == 0))
    def _():
        ql = lax.broadcasted_iota(jnp.int32, (tq, tq), 0)
        kl = lax.broadcasted_iota(jnp.int32, (tq, tq), 1)
        for var in range(2):
            bucket = _t5_bucket(ql - kl + var * tq)
            for h in range(H_ATT):
                t = jnp.zeros((tq, tq), F32)
                for bk in range(NUM_BUCKETS - 1):
                    t = jnp.where(bucket == bk, rb_ref[bk, h] - rb_ref[NUM_BUCKETS - 1, h], t)
                btab[h, var] = t * LOG2E

    qi = qi_ref[0]
    qi8 = jnp.concatenate(
        [qi[blk * tq:(blk + 1) * tq, m * LANES:(m + 1) * LANES]
         for blk in range(nb) for m in range(pairs)], axis=0)
    wi = wi_ref[0]
    for blk in range(nb):
        for h in range(H_IDX):
            wib[blk, h] = jnp.broadcast_to(wi[blk * tq:(blk + 1) * tq, h:h + 1], (tq, tk))
    row_q = lax.broadcasted_iota(jnp.int32, (tq, tk), 0)
    lane_k = lax.broadcasted_iota(jnp.int32, (tq, tk), 1)

    def score_tile(kt, carry, *, last):
        k0 = pl.multiple_of(kt * tk, tk)
        da = jnp.maximum(_dot_nt(qi8, kia_ref[0, pl.ds(k0, tk), :]), 0.0)
        db = jnp.maximum(_dot_nt(qi8, kib_ref[0, pl.ds(k0, tk), :]), 0.0)
        for blk in range(nb):
            s = jnp.zeros((tq, tk), F32)
            for m in range(pairs):
                r0 = (blk * pairs + m) * tq
                s = s + wib[blk, 2 * m] * da[r0:r0 + tq]
                s = s + wib[blk, 2 * m + 1] * db[r0:r0 + tq]
            if last:
                s = jnp.where(lane_k <= row_q + blk * tq, s, NEG_INF)
            sc_std[blk, kt] = s
            sc_t[kt, :, blk * tq:(blk + 1) * tq] = s.T
        return carry

    def score_pair(j, carry):
        score_tile(2 * j, carry, last=False)
        return score_tile(2 * j + 1, carry, last=False)

    lax.fori_loop(0, lax.shift_right_logical(step, 1), score_pair, 0)

    @pl.when((step & 1) == 1)
    def _():
        score_tile(step - 1, 0, last=False)

    score_tile(step, 0, last=True)

    st = (SUBLANES, nb * tq)
    p_lane = step * (nb * tq) + lax.broadcasted_iota(jnp.int32, st, 1)
    kq = jnp.minimum(topk, p_lane + 1).astype(F32)

    def search(n):
        def over_tiles(fn, op):
            pair = {jnp.sum: jnp.add, jnp.max: jnp.maximum, jnp.min: jnp.minimum}[op]
            parts = [op(fn(sc_t[kt].reshape(tk // SUBLANES, SUBLANES, nb * tq), kt), axis=0)
                     for kt in range(n)]
            while len(parts) > 1:
                parts = [pair(*parts[i:i + 2]) if i + 1 < len(parts) else parts[i]
                         for i in range(0, len(parts), 2)]
            x = parts[0]
            for s in (4, 2, 1):
                x = pair(x, pltpu.roll(x, s, 0))
            return x

        def count_ge(th):
            return over_tiles(lambda v, kt: jnp.where(v >= th, 1.0, 0.0), jnp.sum)

        mx = over_tiles(lambda v, kt: v, jnp.max)
        lo0 = over_tiles(lambda v, kt: jnp.where(v == NEG_INF, POS_INF, v), jnp.min)

        def bisect(_, c):
            lo, hix, chi = c
            hib = jnp.where(hix == POS_INF, mx, hix)
            mid = 0.5 * lo + 0.5 * hib
            cnt = count_ge(mid)
            ge = cnt >= kq
            return jnp.where(ge, mid, lo), jnp.where(ge, hix, mid), jnp.where(ge, chi, cnt)

        _, hix, chi = lax.fori_loop(
            0, N_BISECT, bisect,
            (lo0, jnp.full(st, POS_INF, F32), jnp.zeros(st, F32)))

        def sweep(cand):
            cs, ms = [], []
            for kt in range(n):
                v = sc_t[kt].reshape(tk // SUBLANES, SUBLANES, nb * tq)
                ge = v >= cand
                cs.append(jnp.sum(jnp.where(ge, 1.0, 0.0), axis=0))
                ms.append(jnp.max(jnp.where(ge, NEG_INF, v), axis=0))
            c, m = functools.reduce(jnp.add, cs), functools.reduce(jnp.maximum, ms)
            for s in (4, 2, 1):
                c = c + pltpu.roll(c, s, 0)
                m = jnp.maximum(m, pltpu.roll(m, s, 0))
            return c, m

        def snap_cond(c):
            return jnp.max(c[5]) > 0.0

        def snap_body(c):
            cand, chi, t, cge, ngt, pending = c
            ct, nxt = sweep(cand)
            act = pending > 0.0
            ok = ct >= kq
            t = jnp.where(act, cand, t)
            cge = jnp.where(act, ct, cge)
            ngt = jnp.where(act, chi, ngt)
            pending = jnp.where(act & ok, 0.0, pending)
            cand = jnp.where(act & ~ok, nxt, cand)
            chi = jnp.where(act & ~ok, ct, chi)
            return cand, chi, t, cge, ngt, pending

        cand0 = over_tiles(lambda v, kt: jnp.where(v < hix, v, NEG_INF), jnp.max)
        zero = jnp.zeros(st, F32)
        _, _, t_k, cge, ngt, _ = lax.while_loop(
            snap_cond, snap_body, (cand0, chi, zero, zero, zero, jnp.ones(st, F32)))
        thr[...] = t_k
        jthr[...] = jnp.full(st, float(seq), F32)

        need = kq - ngt
        excess = (cge - ngt) > need

        @pl.when(jnp.max(jnp.where(excess, 1.0, 0.0)) > 0.0)
        def _():
            sub_k = lax.broadcasted_iota(jnp.int32, (tk, nb * tq), 0).astype(F32).reshape(
                tk // SUBLANES, SUBLANES, nb * tq)

            def jbisect(_, c):
                jlo, jhi = c
                mid = jnp.floor(0.5 * (jlo + jhi))
                cnt = over_tiles(
                    lambda v, kt: jnp.where((v == t_k) & (sub_k + float(kt * tk) <= mid), 1.0, 0.0),
                    jnp.sum)
                ge = cnt >= need
                return jnp.where(ge, jlo, mid), jnp.where(ge, mid, jhi)

            n_pass = max(1, int(math.ceil(math.log2(seq))))
            _, jhi = lax.fori_loop(
                0, n_pass, jbisect,
                (jnp.full(st, -1.0, F32), jnp.full(st, float(seq - 1), F32)))
            jthr[...] = jhi

    for n in range(1, seq // tk + 1):
        pl.when(n_kt == n)(functools.partial(search, n))

    for blk in range(nb):
        for i, ref in enumerate((thr, jthr)):
            v = jnp.concatenate([ref[:, blk * tq:(blk + 1) * tq]] * (tq // SUBLANES), axis=0).T
            tj[blk, i] = jnp.concatenate([v] * (tk // tq), axis=1)

    lane_kf = lane_k.astype(F32)

    def attend_tile(kt, carry=None, *, kind, first=False):
        qall = jnp.concatenate(
            [qlat_ref[0, :, blk * tq:(blk + 1) * tq, :].reshape(hq, D_LAT) for blk in range(nb)],
            axis=0)
        k0 = pl.multiple_of(kt * tk, tk)
        ct = caug_ref[0, pl.ds(k0, tk), :]
        src = _dot_nt(qall, ct[:, :D_LAT])
        kf = lane_kf + (kt * tk).astype(F32)
        for blk in range(nb):
            s = sc_std[blk, kt]
            tb = tj[blk, 0]
            sel = (s > tb) | ((s == tb) & (kf <= tj[blk, 1]))
            ma = jnp.where(sel, 0.0, NEG_INF)
            for h in range(H_ATT):
                r0 = (blk * H_ATT + h) * tq
                lh = src[r0:r0 + tq, :] + ma
                if kind == "near" and blk == 0:
                    lh = lh + jnp.concatenate([jnp.zeros((tq, tq), F32), btab[h, 1]], axis=1)
                elif kind == "last" and blk == 0:
                    lh = lh + jnp.concatenate([btab[h, 0], jnp.zeros((tq, tq), F32)], axis=1)
                elif kind == "last" and blk == 1:
                    lh = lh + jnp.concatenate([btab[h, 1], btab[h, 0]], axis=1)
                mt = jnp.maximum(lh[:, :tq], lh[:, tq:])
                rm = jnp.broadcast_to(jnp.max(mt, axis=1, keepdims=True), (tq, tq))
                if first:
                    m_new = jnp.maximum(rm, M_INIT)
                else:
                    m_old = mrun[blk, h]
                    m_new = jnp.maximum(m_old, rm)
                    alph[blk, h] = jnp.exp2(m_old - m_new)
                mrun[blk, h] = m_new
                m2 = jnp.concatenate([m_new, m_new], axis=1)
                p_s[r0:r0 + tq, :] = jnp.exp2(lh - m2).astype(BF16)
        pv = _dot(p_s[...], ct)
        if first:
            acc[...] = pv
            return carry
        for blk in range(nb):
            for h in range(H_ATT):
                r0 = (blk * H_ATT + h) * tq
                al = alph[blk, h]
                acc[r0:r0 + tq, :] = acc[r0:r0 + tq, :] * jnp.concatenate([al, al], axis=1) + pv[r0:r0 + tq]
        return carry

    attend_tile(step, kind="last", first=True)

    @pl.when(step >= 1)
    def _():
        attend_tile(step - 1, kind="near")

    lax.fori_loop(0, jnp.maximum(step - 1, 0), functools.partial(attend_tile, kind="far"), 0)

    for m in range(H_ATT // 2):
        rows = []
        for blk in range(nb):
            halves = []
            for h in (2 * m, 2 * m + 1):
                r0 = (blk * H_ATT + h) * tq
                a = acc[r0:r0 + tq, :]
                halves.append(a[:, :D_LAT] * (1.0 / a[:, D_LAT:]))
            rows.append(jnp.concatenate(halves, axis=1))
        pair = jnp.concatenate(rows, axis=0).astype(BF16)
        o_ref[0, :, m * 2 * D_V:(m + 1) * 2 * D_V] = _dot(pair, wuv_ref[m]).astype(BF16)


def _attention(rel_bias, qlat, qi, wi, caug, kia, kib, wuv2, *, topk):
    b, _, s, _ = qlat.shape
    tq, tk = Q_BLOCK, K_TILE
    assert s % tk == 0 and tk == 2 * tq
    nb = tk // tq
    n_kt = s // tk
    hq = H_ATT * tq
    di = H_IDX * D_IDX
    dv = H_ATT * D_V
    qrow = lambda i, j: (i, j, 0)
    full = lambda i, j: (i, 0, 0)
    return pl.pallas_call(
        functools.partial(_attn_body, topk=topk, seq=s),
        grid=(b, s // tk),
        in_specs=[
            pl.BlockSpec(memory_space=pltpu.SMEM),
            pl.BlockSpec((1, H_ATT, tk, D_LAT), lambda i, j: (i, 0, j, 0)),
            pl.BlockSpec((1, tk, di), qrow),
            pl.BlockSpec((1, tk, LANES), qrow),
            pl.BlockSpec((1, s, 2 * D_LAT), full),
            pl.BlockSpec((1, s, LANES), full),
            pl.BlockSpec((1, s, LANES), full),
            _const_spec((H_ATT // 2, 2 * D_LAT, 2 * D_V)),
        ],
        out_specs=pl.BlockSpec((1, tk, dv), qrow),
        out_shape=jax.ShapeDtypeStruct((b, s, dv), BF16),
        scratch_shapes=[
            pltpu.VMEM((nb, n_kt, tq, tk), F32),
            pltpu.VMEM((n_kt, tk, nb * tq), F32),
            pltpu.VMEM((nb * hq, tk), F32),
            pltpu.VMEM((nb, H_ATT, tq, tq), F32),
            pltpu.VMEM((nb, H_ATT, tq, tq), F32),
            pltpu.VMEM((H_ATT, 2, tq, tq), F32),
            pltpu.VMEM((nb, H_IDX, tq, tk), F32),
            pltpu.VMEM((nb * hq, tk), BF16),
            pltpu.VMEM((nb, 2, tq, tk), F32),
            pltpu.VMEM((SUBLANES, nb * tq), F32),
            pltpu.VMEM((SUBLANES, nb * tq), F32),
        ],
        compiler_params=pltpu.CompilerParams(
            dimension_semantics=("arbitrary", "arbitrary"), vmem_limit_bytes=VMEM_LIMIT_BYTES),
        name="dsa_attention",
    )(rel_bias, qlat, qi, wi, caug, kia, kib, wuv2)


def _block_diag_pairs(w):
    h, a, b = w.shape
    w = w.reshape(h // 2, 2, a, b)
    z = jnp.zeros((h // 2, a, b), w.dtype)
    top = jnp.concatenate([w[:, 0], z], axis=2)
    bot = jnp.concatenate([z, w[:, 1]], axis=2)
    return jnp.concatenate([top, bot], axis=1)


def _pack_att_w(w_in):
    dq = H_ATT * D_QK
    di = H_IDX * D_IDX
    d = w_in.shape[0]
    o = dq + D_LAT + di
    w_ki = w_in[:, o:o + D_IDX]
    w_wi = w_in[:, o + D_IDX:o + D_IDX + H_IDX]
    z_ki = jnp.zeros((d, LANES - D_IDX), w_in.dtype)
    z_wi = jnp.zeros((d, LANES - H_IDX), w_in.dtype)
    return jnp.concatenate([w_in[:, :o], w_ki, z_ki, z_ki, w_ki, w_wi, z_wi], axis=1)


def kernel(x, norm_mix_g, norm_mlp_g, final_norm_g, rec_w_in, rec_conv_w, rec_conv_b, rec_w_a, rec_b_a, rec_w_x, rec_b_x, rec_lambda, rec_w_out, att_w_in, att_kv_norm_g, att_w_uk, att_w_uv, att_w_o, rel_bias, mlp_w_up, mlp_w_down):
    b, s, d = x.shape
    depth = norm_mix_g.shape[0]
    topk = min(TOPK_MAX, s // 4)
    row = lambda v: v.reshape(1, -1).astype(F32)
    for layer in range(depth):
        j = layer // 2
        g_mix = row(norm_mix_g[layer])
        if layer % 2 == 0:
            y = _rec_mixer(
                x, g_mix, rec_w_in[j].astype(BF16), rec_conv_w[j].astype(F32), row(rec_conv_b[j]),
                rec_w_a[j].astype(BF16), row(rec_b_a[j]), rec_w_x[j].astype(BF16), row(rec_b_x[j]),
                row(rec_lambda[j]))
            w_pre = rec_w_out[j]
        else:
            qlat, caug, qi, kia, kib, wi = _att_proj(
                x, g_mix, _pack_att_w(att_w_in[j]).astype(BF16), row(att_kv_norm_g[j]),
                _block_diag_pairs(att_w_uk[j]).astype(BF16))
            y = _attention(
                rel_bias.astype(F32), qlat, qi, wi, caug, kia, kib,
                _block_diag_pairs(att_w_uv[j]).astype(BF16), topk=topk)
            w_pre = att_w_o[j]
        x = _mlp_layer(
            x.reshape(b * s, d), y.reshape(b * s, d), w_pre.astype(BF16), row(norm_mlp_g[layer]),
            mlp_w_up[layer].astype(BF16), mlp_w_down[layer].astype(BF16), row(final_norm_g),
            final=(layer == depth - 1)).reshape(b, s, d)
    return x
```

```python
import functools
import math

import jax
import jax.numpy as jnp
from jax import lax
from jax.experimental import pallas as pl
from jax.experimental.pallas import tpu as pltpu

F32 = jnp.float32
BF16 = jnp.bfloat16

EPS = 1e-6
RG_BLOCKS = 4
CONV_W = 4
RG_C = 8.0
H_ATT = 16
D_QK = 64
D_V = 64
D_LAT = 128
H_IDX = 8
D_IDX = 64
TOPK_MAX = 256
ATT_SCALE = D_QK ** -0.5
LOG2E = math.log2(math.e)
IDX_W_SCALE = (H_IDX ** -0.5) * (D_IDX ** -0.5)
NUM_BUCKETS = 32
MAX_DISTANCE = 128
MAX_EXACT = NUM_BUCKETS // 2

LANES = 128
SUBLANES = 8
VMEM_LIMIT_BYTES = 56 * 1024 * 1024

Q_BLOCK = 128
K_TILE = 256
N_BISECT = 14
NEG_INF = float("-inf")
POS_INF = float("inf")
M_INIT = -1e30


def _rms(x, g):
    ms = jnp.mean(x * x, axis=-1, keepdims=True)
    return x * lax.rsqrt(ms + EPS) * g


def _dot(a, b):
    return jnp.dot(a, b, preferred_element_type=F32)


def _dot_nt(a, b):
    return lax.dot_general(a, b, (((1,), (1,)), ((), ())), preferred_element_type=F32)


def _const_spec(shape):
    nd = len(shape)
    return pl.BlockSpec(shape, lambda *_: (0,) * nd, pipeline_mode=pl.Buffered(1))


def _mlp_tile(x, y, wpre_ref, g_ref, wup_ref, wdn_ref, gfin_ref, *, ff_chunk, final):
    x1 = x + _dot(y, wpre_ref[...])
    hn = _rms(x1, g_ref[...]).astype(BF16)
    acc = x1
    d_ff = wup_ref.shape[1]
    for c in range(d_ff // ff_chunk):
        h = _dot(hn, wup_ref[:, c * ff_chunk:(c + 1) * ff_chunk])
        h = jnp.maximum(h, 0.0)
        h = (h * h).astype(BF16)
        acc = acc + _dot(h, wdn_ref[c * ff_chunk:(c + 1) * ff_chunk, :])
    if final:
        acc = _rms(acc, gfin_ref[...])
    return acc


def _mlp_body(x_ref, y_ref, wpre_ref, g_ref, wup_ref, wdn_ref, gfin_ref, o_ref, *, ff_chunk, final):
    o_ref[...] = _mlp_tile(x_ref[...], y_ref[...], wpre_ref, g_ref, wup_ref, wdn_ref, gfin_ref,
                           ff_chunk=ff_chunk, final=final)


def _mlp_layer(x2d, y2d, w_pre, g, w_up, w_dn, g_fin, *, final, tm=512, ff_chunk=512):
    n, d = x2d.shape
    d_ff = w_up.shape[1]
    tm = min(tm, n)
    return pl.pallas_call(
        functools.partial(_mlp_body, ff_chunk=min(ff_chunk, d_ff), final=final),
        grid=(n // tm,),
        in_specs=[
            pl.BlockSpec((tm, d), lambda i: (i, 0)),
            pl.BlockSpec((tm, d), lambda i: (i, 0)),
            _const_spec((d, d)),
            _const_spec((1, d)),
            _const_spec((d, d_ff)),
            _const_spec((d_ff, d)),
            _const_spec((1, d)),
        ],
        out_specs=pl.BlockSpec((tm, d), lambda i: (i, 0)),
        out_shape=jax.ShapeDtypeStruct((n, d), F32),
        compiler_params=pltpu.CompilerParams(
            dimension_semantics=("arbitrary",), vmem_limit_bytes=VMEM_LIMIT_BYTES),
        name="mlp_block",
    )(x2d, y2d, w_pre, g, w_up, w_dn, g_fin)


def _gelu_tanh(x):
    c = math.sqrt(2.0 / math.pi)
    return 0.5 * x * (1.0 + jnp.tanh(c * (x + 0.044715 * (x * x * x))))


def _sigmoid(x):
    return 1.0 / (1.0 + jnp.exp(-x))


def _rec_block(n, r0, rh, y2, cw_ref, cb_ref, wa_ref, ba_ref, wx_ref, bx_ref, lam_ref,
               xbuf, hcar, y_out, *, ts):
    d = y2.shape[-1] // 2
    bw = d // RG_BLOCKS
    cs = slice(n * bw, (n + 1) * bw)
    rs = slice(r0, r0 + rh)
    gate = _gelu_tanh(y2[rs, cs])
    xr = y2[rs, d + n * bw:d + (n + 1) * bw]

    xbuf[SUBLANES + r0:SUBLANES + r0 + rh, cs] = xr
    cw = cw_ref[:, cs]
    xc = cb_ref[:, cs] + xbuf[pl.ds(SUBLANES + r0 - 3, rh), cs] * cw[0:1]
    xc = xc + xbuf[pl.ds(SUBLANES + r0 - 2, rh), cs] * cw[1:2]
    xc = xc + xbuf[pl.ds(SUBLANES + r0 - 1, rh), cs] * cw[2:3]
    xc = xc + xr * cw[3:4]
    if r0 + rh == ts:
        xbuf[0:SUBLANES, cs] = xbuf[ts:ts + SUBLANES, cs]

    xcb = xc.astype(BF16)
    r = _sigmoid(_dot(xcb, wa_ref[n]) + ba_ref[:, cs])
    ig = _sigmoid(_dot(xcb, wx_ref[n]) + bx_ref[:, cs])
    nl = -lam_ref[:, cs]
    softplus = jnp.maximum(nl, 0.0) + jnp.log1p(jnp.exp(-jnp.abs(nl)))
    log_a = (-RG_C * r) * softplus
    a = jnp.exp(log_a)
    v = -jnp.tanh(log_a) * (a * a + 1.0)
    root = jnp.where(v > 0.0, v * lax.rsqrt(v), 0.0)
    u = root * (ig * xc)

    row = lax.broadcasted_iota(jnp.int32, (SUBLANES, bw), 0)
    masks = [(s, row >= s) for s in (1, 2, 4)]
    hp = hcar[:, cs]
    hs = []
    for c in range(rh // SUBLANES):
        av = a[c * SUBLANES:(c + 1) * SUBLANES]
        uv = u[c * SUBLANES:(c + 1) * SUBLANES]
        for s, m in masks:
            a_sh = jnp.where(m, pltpu.roll(av, s, 0), 1.0)
            u_sh = jnp.where(m, pltpu.roll(uv, s, 0), 0.0)
            uv = av * u_sh + uv
            av = av * a_sh
        hv = av * hp + uv
        hs.append(hv)
        hp = hv[SUBLANES - 1:SUBLANES, :]
    hcar[:, cs] = hp
    y_out[rs, cs] = (jnp.concatenate(hs, axis=0) * gate).astype(BF16)


def _rec_mlp_body(x_ref, xp_ref, g_ref, win_ref, cw_ref, cb_ref, wa_ref, ba_ref, wx_ref, bx_ref,
                  lam_ref, wpre_ref, gm_ref, wup_ref, wdn_ref, gfin_ref, o_ref,
                  xbuf, hcar, y_s, *, ts, tiles_per_seq, ff_chunk, final):
    t = pl.program_id(0)
    d = x_ref.shape[-1]
    d_ff = wup_ref.shape[1]
    n_chunks = d_ff // ff_chunk
    assert n_chunks % RG_BLOCKS == 0

    @pl.when(lax.rem(t, tiles_per_seq) == 0)
    def _():
        xbuf[0:SUBLANES, :] = jnp.zeros((SUBLANES, d), F32)
        hcar[...] = jnp.zeros((1, d), F32)

    @pl.when(t == 0)
    def _():
        y_s[...] = jnp.zeros(y_s.shape, BF16)

    y2 = _dot(_rms(x_ref[...], g_ref[...]).astype(BF16), win_ref[...])
    x1 = xp_ref[...] + _dot(y_s[...], wpre_ref[...])
    hn = _rms(x1, gm_ref[...]).astype(BF16)
    acc = x1
    per_block = n_chunks // RG_BLOCKS
    rh = ts // per_block
    for n in range(RG_BLOCKS):
        for k in range(per_block):
            c = n * per_block + k
            h = jnp.maximum(_dot(hn, wup_ref[:, c * ff_chunk:(c + 1) * ff_chunk]), 0.0)
            _rec_block(n, k * rh, rh, y2, cw_ref, cb_ref, wa_ref, ba_ref, wx_ref, bx_ref,
                       lam_ref, xbuf, hcar, y_s, ts=ts)
            acc = acc + _dot((h * h).astype(BF16), wdn_ref[c * ff_chunk:(c + 1) * ff_chunk, :])
    if final:
        acc = _rms(acc, gfin_ref[...])
    o_ref[...] = acc


def _rec_mlp_layer(x, g, w_in, conv_w, conv_b, w_a, b_a, w_x, b_x, lam,
                   w_pre, g_mlp, w_up, w_dn, g_fin, *, final, ts=256, ff_chunk=512):
    b, s, d = x.shape
    ts = min(ts, s)
    bw = d // RG_BLOCKS
    d_ff = w_up.shape[1]
    n_tiles = b * s // ts
    x2d = x.reshape(b * s, d)
    cur = lambda t: (jnp.minimum(t, n_tiles - 1), 0)
    prev = lambda t: (jnp.maximum(t - 1, 0), 0)
    out = pl.pallas_call(
        functools.partial(_rec_mlp_body, ts=ts, tiles_per_seq=s // ts,
                          ff_chunk=min(ff_chunk, d_ff), final=final),
        grid=(n_tiles + 1,),
        in_specs=[
            pl.BlockSpec((ts, d), cur),
            pl.BlockSpec((ts, d), prev),
            _const_spec((1, d)),
            _const_spec((d, 2 * d)),
            _const_spec((CONV_W, d)),
            _const_spec((1, d)),
            _const_spec((RG_BLOCKS, bw, bw)),
            _const_spec((1, d)),
            _const_spec((RG_BLOCKS, bw, bw)),
            _const_spec((1, d)),
            _const_spec((1, d)),
            _const_spec((d, d)),
            _const_spec((1, d)),
            _const_spec((d, d_ff)),
            _const_spec((d_ff, d)),
            _const_spec((1, d)),
        ],
        out_specs=pl.BlockSpec((ts, d), prev),
        out_shape=jax.ShapeDtypeStruct((b * s, d), F32),
        scratch_shapes=[
            pltpu.VMEM((ts + SUBLANES, d), F32),
            pltpu.VMEM((1, d), F32),
            pltpu.VMEM((ts, d), BF16),
        ],
        compiler_params=pltpu.CompilerParams(
            dimension_semantics=("arbitrary",), vmem_limit_bytes=VMEM_LIMIT_BYTES,
        ),
        name="rglru_mlp",
    )(x2d, x2d, g, w_in, conv_w, conv_b, w_a, b_a, w_x, b_x, lam, w_pre, g_mlp, w_up, w_dn, g_fin)
    return out.reshape(b, s, d)


def _attproj_body(x_ref, g_ref, w_ref, gkv_ref, wuk_ref,
                  qlat_ref, caug_ref, qi_ref, kia_ref, kib_ref, wi_ref):
    dq = H_ATT * D_QK
    di = H_IDX * D_IDX
    hn = _rms(x_ref[0], g_ref[...]).astype(BF16)
    y = _dot(hn, w_ref[...])
    ts = y.shape[0]
    o = dq
    craw = y[:, o:o + D_LAT]
    o += D_LAT
    qi_ref[0] = y[:, o:o + di].astype(BF16)
    o += di
    kia_ref[0] = y[:, o:o + LANES].astype(BF16)
    o += LANES
    kib_ref[0] = y[:, o:o + LANES].astype(BF16)
    o += LANES
    wi_ref[0] = y[:, o:o + LANES] * IDX_W_SCALE

    c = _rms(craw, gkv_ref[...])
    caug_ref[0] = jnp.concatenate([c, jnp.ones((ts, D_LAT), F32)], axis=1).astype(BF16)

    for m in range(H_ATT // 2):
        qp = y[:, m * LANES:(m + 1) * LANES].astype(BF16)
        ql = _dot(qp, wuk_ref[m]) * (ATT_SCALE * LOG2E)
        qlat_ref[0, 2 * m] = ql[:, :D_LAT].astype(BF16)
        qlat_ref[0, 2 * m + 1] = ql[:, D_LAT:].astype(BF16)


def _att_proj(x, g, w_pack, g_kv, wuk2, *, ts=256):
    b, s, d = x.shape
    ts = min(ts, s)
    n_out = w_pack.shape[1]
    di = H_IDX * D_IDX
    row = lambda i, j: (i, j, 0)
    return pl.pallas_call(
        _attproj_body,
        grid=(b, s // ts),
        in_specs=[
            pl.BlockSpec((1, ts, d), row),
            _const_spec((1, d)),
            _const_spec((d, n_out)),
            _const_spec((1, D_LAT)),
            _const_spec((H_ATT // 2, 2 * D_QK, 2 * D_LAT)),
        ],
        out_specs=[
            pl.BlockSpec((1, H_ATT, ts, D_LAT), lambda i, j: (i, 0, j, 0)),
            pl.BlockSpec((1, ts, 2 * D_LAT), row),
            pl.BlockSpec((1, ts, di), row),
            pl.BlockSpec((1, ts, LANES), row),
            pl.BlockSpec((1, ts, LANES), row),
            pl.BlockSpec((1, ts, LANES), row),
        ],
        out_shape=[
            jax.ShapeDtypeStruct((b, H_ATT, s, D_LAT), BF16),
            jax.ShapeDtypeStruct((b, s, 2 * D_LAT), BF16),
            jax.ShapeDtypeStruct((b, s, di), BF16),
            jax.ShapeDtypeStruct((b, s, LANES), BF16),
            jax.ShapeDtypeStruct((b, s, LANES), BF16),
            jax.ShapeDtypeStruct((b, s, LANES), F32),
        ],
        compiler_params=pltpu.CompilerParams(
            dimension_semantics=("arbitrary", "arbitrary"), vmem_limit_bytes=VMEM_LIMIT_BYTES),
        name="att_proj",
    )(x, g, w_pack, g_kv, wuk2)


def _t5_bucket(dist):
    n = jnp.maximum(dist, 0)
    nf = jnp.maximum(n, 1).astype(F32)
    large = MAX_EXACT + (jnp.log(nf / MAX_EXACT) / math.log(MAX_DISTANCE / MAX_EXACT)
                         * (NUM_BUCKETS - MAX_EXACT)).astype(jnp.int32)
    large = jnp.minimum(large, NUM_BUCKETS - 1)
    return jnp.where(n < MAX_EXACT, n, large)


def _attn_body(rb_ref, qlat_ref, qi_ref, wi_ref, caug_ref, kia_ref, kib_ref, wuv_ref, o_ref,
               sc_std, sc_t, acc, mrun, alph, btab, wib, p_s, tj, thr, jthr, *, topk, seq):
    tq, tk = Q_BLOCK, K_TILE
    nb = tk // tq
    step = pl.program_id(1)
    n_kt = step + 1
    hq = H_ATT * tq
    pairs = H_IDX // 2

    @pl.when((pl.program_id(0) == 0) & (step == 0))
    def _():
        ql = lax.broadcasted_iota(jnp.int32, (tq, tq), 0)
        kl = lax.broadcasted_iota(jnp.int32, (tq, tq), 1)
        for var in range(2):
            bucket = _t5_bucket(ql - kl + var * tq)
            for h in range(H_ATT):
                t = jnp.zeros((tq, tq), F32)
                for bk in range(NUM_BUCKETS - 1):
                    t = jnp.where(bucket == bk, rb_ref[bk, h] - rb_ref[NUM_BUCKETS - 1, h], t)
                btab[h, var] = t * LOG2E

    qi = qi_ref[0]
    qi8 = jnp.concatenate(
        [qi[blk * tq:(blk + 1) * tq, m * LANES:(m + 1) * LANES]
         for blk in range(nb) for m in range(pairs)], axis=0)
    wi = wi_ref[0]
    for blk in range(nb):
        for h in range(H_IDX):
            wib[blk, h] = jnp.broadcast_to(wi[blk * tq:(blk + 1) * tq, h:h + 1], (tq, tk))
    row_q = lax.broadcasted_iota(jnp.int32, (tq, tk), 0)
    lane_k = lax.broadcasted_iota(jnp.int32, (tq, tk), 1)

    def score_tile(kt, carry, *, last):
        k0 = pl.multiple_of(kt * tk, tk)
        da = jnp.maximum(_dot_nt(qi8, kia_ref[0, pl.ds(k0, tk), :]), 0.0)
        db = jnp.maximum(_dot_nt(qi8, kib_ref[0, pl.ds(k0, tk), :]), 0.0)
        for blk in range(nb):
            s = jnp.zeros((tq, tk), F32)
            for m in range(pairs):
                r0 = (blk * pairs + m) * tq
                s = s + wib[blk, 2 * m] * da[r0:r0 + tq]
                s = s + wib[blk, 2 * m + 1] * db[r0:r0 + tq]
            if last:
                s = jnp.where(lane_k <= row_q + blk * tq, s, NEG_INF)
            sc_std[blk, kt] = s
            sc_t[kt, :, blk * tq:(blk + 1) * tq] = s.T
        return carry

    def score_pair(j, carry):
        score_tile(2 * j, carry, last=False)
        return score_tile(2 * j + 1, carry, last=False)

    lax.fori_loop(0, lax.shift_right_logical(step, 1), score_pair, 0)

    @pl.when((step & 1) == 1)
    def _():
        score_tile(step - 1, 0, last=False)

    score_tile(step, 0, last=True)

    st = (SUBLANES, nb * tq)
    p_lane = step * (nb * tq) + lax.broadcasted_iota(jnp.int32, st, 1)
    kq = jnp.minimum(topk, p_lane + 1).astype(F32)

    def search(n):
        def over_tiles(fn, op):
            pair = {jnp.sum: jnp.add, jnp.max: jnp.maximum, jnp.min: jnp.minimum}[op]
            parts = [op(fn(sc_t[kt].reshape(tk // SUBLANES, SUBLANES, nb * tq), kt), axis=0)
                     for kt in range(n)]
            while len(parts) > 1:
                parts = [pair(*parts[i:i + 2]) if i + 1 < len(parts) else parts[i]
                         for i in range(0, len(parts), 2)]
            x = parts[0]
            for s in (4, 2, 1):
                x = pair(x, pltpu.roll(x, s, 0))
            return x

        def count_ge(th):
            return over_tiles(lambda v, kt: jnp.where(v >= th, 1.0, 0.0), jnp.sum)

        mx = over_tiles(lambda v, kt: v, jnp.max)
        lo0 = over_tiles(lambda v, kt: jnp.where(v == NEG_INF, POS_INF, v), jnp.min)

        def bisect(_, c):
            lo, hix, chi = c
            hib = jnp.where(hix == POS_INF, mx, hix)
            mid = 0.5 * lo + 0.5 * hib
            cnt = count_ge(mid)
            ge = cnt >= kq
            return jnp.where(ge, mid, lo), jnp.where(ge, hix, mid), jnp.where(ge, chi, cnt)

        _, hix, chi = lax.fori_loop(
            0, N_BISECT, bisect,
            (lo0, jnp.full(st, POS_INF, F32), jnp.zeros(st, F32)))

        def sweep(cand):
            cs, ms = [], []
            for kt in range(n):
                v = sc_t[kt].reshape(tk // SUBLANES, SUBLANES, nb * tq)
                ge = v >= cand
                cs.append(jnp.sum(jnp.where(ge, 1.0, 0.0), axis=0))
                ms.append(jnp.max(jnp.where(ge, NEG_INF, v), axis=0))
            c, m = functools.reduce(jnp.add, cs), functools.reduce(jnp.maximum, ms)
            for s in (4, 2, 1):
                c = c + pltpu.roll(c, s, 0)
                m = jnp.maximum(m, pltpu.roll(m, s, 0))
            return c, m

        def snap_cond(c):
            return jnp.max(c[5]) > 0.0

        def snap_body(c):
            cand, chi, t, cge, ngt, pending = c
            ct, nxt = sweep(cand)
            act = pending > 0.0
            ok = ct >= kq
            t = jnp.where(act, cand, t)
            cge = jnp.where(act, ct, cge)
            ngt = jnp.where(act, chi, ngt)
            pending = jnp.where(act & ok, 0.0, pending)
            cand = jnp.where(act & ~ok, nxt, cand)
            chi = jnp.where(act & ~ok, ct, chi)
            return cand, chi, t, cge, ngt, pending

        cand0 = over_tiles(lambda v, kt: jnp.where(v < hix, v, NEG_INF), jnp.max)
        zero = jnp.zeros(st, F32)
        _, _, t_k, cge, ngt, _ = lax.while_loop(
            snap_cond, snap_body, (cand0, chi, zero, zero, zero, jnp.ones(st, F32)))
        thr[...] = t_k
        jthr[...] = jnp.full(st, float(seq), F32)

        need = kq - ngt
        excess = (cge - ngt) > need

        @pl.when(jnp.max(jnp.where(excess, 1.0, 0.0)) > 0.0)
        def _():
            sub_k = lax.broadcasted_iota(jnp.int32, (tk, nb * tq), 0).astype(F32).reshape(
                tk // SUBLANES, SUBLANES, nb * tq)

            def jbisect(_, c):
                jlo, jhi = c
                mid = jnp.floor(0.5 * (jlo + jhi))
                cnt = over_tiles(
                    lambda v, kt: jnp.where((v == t_k) & (sub_k + float(kt * tk) <= mid), 1.0, 0.0),
                    jnp.sum)
                ge = cnt >= need
                return jnp.where(ge, jlo, mid), jnp.where(ge, mid, jhi)

            n_pass = max(1, int(math.ceil(math.log2(seq))))
            _, jhi = lax.fori_loop(
                0, n_pass, jbisect,
                (jnp.full(st, -1.0, F32), jnp.full(st, float(seq - 1), F32)))
            jthr[...] = jhi

    for n in range(1, seq // tk + 1):
        pl.when(n_kt == n)(functools.partial(search, n))

    for blk in range(nb):
        for i, ref in enumerate((thr, jthr)):
            v = jnp.concatenate([ref[:, blk * tq:(blk + 1) * tq]] * (tq // SUBLANES), axis=0).T
            tj[blk, i] = jnp.concatenate([v] * (tk // tq), axis=1)

    lane_kf = lane_k.astype(F32)

    def attend_tile(kt, carry=None, *, kind, first=False):
        qall = jnp.concatenate(
            [qlat_ref[0, :, blk * tq:(blk + 1) * tq, :].reshape(hq, D_LAT) for blk in range(nb)],
            axis=0)
        k0 = pl.multiple_of(kt * tk, tk)
        ct = caug_ref[0, pl.ds(k0, tk), :]
        src = _dot_nt(qall, ct[:, :D_LAT])
        kf = lane_kf + (kt * tk).astype(F32)
        for blk in range(nb):
            s = sc_std[blk, kt]
            tb = tj[blk, 0]
            sel = (s > tb) | ((s == tb) & (kf <= tj[blk, 1]))
            ma = jnp.where(sel, 0.0, NEG_INF)
            for h in range(H_ATT):
                r0 = (blk * H_ATT + h) * tq
                lh = src[r0:r0 + tq, :] + ma
                if kind == "near" and blk == 0:
                    lh = lh + jnp.concatenate([jnp.zeros((tq, tq), F32), btab[h, 1]], axis=1)
                elif kind == "last" and blk == 0:
                    lh = lh + jnp.concatenate([btab[h, 0], jnp.zeros((tq, tq), F32)], axis=1)
                elif kind == "last" and blk == 1:
                    lh = lh + jnp.concatenate([btab[h, 1], btab[h, 0]], axis=1)
                mt = jnp.maximum(lh[:, :tq], lh[:, tq:])
                rm = jnp.broadcast_to(jnp.max(mt, axis=1, keepdims=True), (tq, tq))
                if first:
                    m_new = jnp.maximum(rm, M_INIT)
                else:
                    m_old = mrun[blk, h]
                    m_new = jnp.maximum(m_old, rm)
                    alph[blk, h] = jnp.exp2(m_old - m_new)
                mrun[blk, h] = m_new
                m2 = jnp.concatenate([m_new, m_new], axis=1)
                p_s[r0:r0 + tq, :] = jnp.exp2(lh - m2).astype(BF16)
        pv = _dot(p_s[...], ct)
        if first:
            acc[...] = pv
            return carry
        for blk in range(nb):
            for h in range(H_ATT):
                r0 = (blk * H_ATT + h) * tq
                al = alph[blk, h]
                acc[r0:r0 + tq, :] = acc[r0:r0 + tq, :] * jnp.concatenate([al, al], axis=1) + pv[r0:r0 + tq]
        return carry

    attend_tile(step, kind="last", first=True)

    @pl.when(step >= 1)
    def _():
        attend_tile(step - 1, kind="near")

    lax.fori_loop(0, jnp.maximum(step - 1, 0), functools.partial(attend_tile, kind="far"), 0)

    for m in range(H_ATT // 2):
        rows = []
        for blk in range(nb):
            halves = []
            for h in (2 * m, 2 * m + 1):
                r0 = (blk * H_ATT + h) * tq
                a = acc[r0:r0 + tq, :]
                halves.append(a[:, :D_LAT] * (1.0 / a[:, D_LAT:]))
            rows.append(jnp.concatenate(halves, axis=1))
        pair = jnp.concatenate(rows, axis=0).astype(BF16)
        o_ref[0, :, m * 2 * D_V:(m + 1) * 2 * D_V] = _dot(pair, wuv_ref[m]).astype(BF16)


def _attention(rel_bias, qlat, qi, wi, caug, kia, kib, wuv2, *, topk):
    b, _, s, _ = qlat.shape
    tq, tk = Q_BLOCK, K_TILE
    assert s % tk == 0 and tk == 2 * tq
    nb = tk // tq
    n_kt = s // tk
    hq = H_ATT * tq
    di = H_IDX * D_IDX
    dv = H_ATT * D_V
    qrow = lambda i, j: (i, j, 0)
    full = lambda i, j: (i, 0, 0)
    return pl.pallas_call(
        functools.partial(_attn_body, topk=topk, seq=s),
        grid=(b, s // tk),
        in_specs=[
            pl.BlockSpec(memory_space=pltpu.SMEM),
            pl.BlockSpec((1, H_ATT, tk, D_LAT), lambda i, j: (i, 0, j, 0)),
            pl.BlockSpec((1, tk, di), qrow),
            pl.BlockSpec((1, tk, LANES), qrow),
            pl.BlockSpec((1, s, 2 * D_LAT), full),
            pl.BlockSpec((1, s, LANES), full),
            pl.BlockSpec((1, s, LANES), full),
            _const_spec((H_ATT // 2, 2 * D_LAT, 2 * D_V)),
        ],
        out_specs=pl.BlockSpec((1, tk, dv), qrow),
        out_shape=jax.ShapeDtypeStruct((b, s, dv), BF16),
        scratch_shapes=[
            pltpu.VMEM((nb, n_kt, tq, tk), F32),
            pltpu.VMEM((n_kt, tk, nb * tq), F32),
            pltpu.VMEM((nb * hq, tk), F32),
            pltpu.VMEM((nb, H_ATT, tq, tq), F32),
            pltpu.VMEM((nb, H_ATT, tq, tq), F32),
            pltpu.VMEM((H_ATT, 2, tq, tq), F32),
            pltpu.VMEM((nb, H_IDX, tq, tk), F32),
            pltpu.VMEM((nb * hq, tk), BF16),
            pltpu.VMEM((nb, 2, tq, tk), F32),
            pltpu.VMEM((SUBLANES, nb * tq), F32),
            pltpu.VMEM((SUBLANES, nb * tq), F32),
        ],
        compiler_params=pltpu.CompilerParams(
            dimension_semantics=("arbitrary", "arbitrary"), vmem_limit_bytes=VMEM_LIMIT_BYTES),
        name="dsa_attention",
    )(rel_bias, qlat, qi, wi, caug, kia, kib, wuv2)


def _block_diag_pairs(w):
    h, a, b = w.shape
    w = w.reshape(h // 2, 2, a, b)
    z = jnp.zeros((h // 2, a, b), w.dtype)
    top = jnp.concatenate([w[:, 0], z], axis=2)
    bot = jnp.concatenate([z, w[:, 1]], axis=2)
    return jnp.concatenate([top, bot], axis=1)


def _pack_att_w(w_in):
    dq = H_ATT * D_QK
    di = H_IDX * D_IDX
    d = w_in.shape[0]
    o = dq + D_LAT + di
    w_ki = w_in[:, o:o + D_IDX]
    w_wi = w_in[:, o + D_IDX:o + D_IDX + H_IDX]
    z_ki = jnp.zeros((d, LANES - D_IDX), w_in.dtype)
    z_wi = jnp.zeros((d, LANES - H_IDX), w_in.dtype)
    return jnp.concatenate([w_in[:, :o], w_ki, z_ki, z_ki, w_ki, w_wi, z_wi], axis=1)


def kernel(x, norm_mix_g, norm_mlp_g, final_norm_g, rec_w_in, rec_conv_w, rec_conv_b, rec_w_a, rec_b_a, rec_w_x, rec_b_x, rec_lambda, rec_w_out, att_w_in, att_kv_norm_g, att_w_uk, att_w_uv, att_w_o, rel_bias, mlp_w_up, mlp_w_down):
    b, s, d = x.shape
    depth = norm_mix_g.shape[0]
    topk = min(TOPK_MAX, s // 4)
    row = lambda v: v.reshape(1, -1).astype(F32)
    for layer in range(depth):
        j = layer // 2
        g_mix = row(norm_mix_g[layer])
        final = layer == depth - 1
        mlp_w = (row(norm_mlp_g[layer]), mlp_w_up[layer].astype(BF16), mlp_w_down[layer].astype(BF16),
                 row(final_norm_g))
        if layer % 2 == 0:
            x = _rec_mlp_layer(
                x, g_mix, rec_w_in[j].astype(BF16), rec_conv_w[j].astype(F32), row(rec_conv_b[j]),
                rec_w_a[j].astype(BF16), row(rec_b_a[j]), rec_w_x[j].astype(BF16), row(rec_b_x[j]),
                row(rec_lambda[j]), rec_w_out[j].astype(BF16), *mlp_w, final=final)
        else:
            qlat, caug, qi, kia, kib, wi = _att_proj(
                x, g_mix, _pack_att_w(att_w_in[j]).astype(BF16), row(att_kv_norm_g[j]),
                _block_diag_pairs(att_w_uk[j]).astype(BF16))
            y = _attention(
                rel_bias.astype(F32), qlat, qi, wi, caug, kia, kib,
                _block_diag_pairs(att_w_uv[j]).astype(BF16), topk=topk)
            x = _mlp_layer(
                x.reshape(b * s, d), y.reshape(b * s, d), att_w_o[j].astype(BF16), *mlp_w,
                final=final).reshape(b, s, d)
    return x
```

```python
import functools
import math

import jax
import jax.numpy as jnp
from jax import lax
from jax.experimental import pallas as pl
from jax.experimental.pallas import tpu as pltpu

F32 = jnp.float32
BF16 = jnp.bfloat16

EPS = 1e-6
RG_BLOCKS = 4
CONV_W = 4
RG_C = 8.0
H_ATT = 16
D_QK = 64
D_V = 64
D_LAT = 128
H_IDX = 8
D_IDX = 64
TOPK_MAX = 256
ATT_SCALE = D_QK ** -0.5
LOG2E = math.log2(math.e)
IDX_W_SCALE = (H_IDX ** -0.5) * (D_IDX ** -0.5)
NUM_BUCKETS = 32
MAX_DISTANCE = 128
MAX_EXACT = NUM_BUCKETS // 2

LANES = 128
SUBLANES = 8
VMEM_LIMIT_BYTES = 56 * 1024 * 1024

Q_BLOCK = 128
K_TILE = 256
N_BISECT = 14
NEG_INF = float("-inf")
POS_INF = float("inf")
M_INIT = -1e30


def _rms(x, g):
    ms = jnp.mean(x * x, axis=-1, keepdims=True)
    return x * lax.rsqrt(ms + EPS) * g


def _dot(a, b):
    return jnp.dot(a, b, preferred_element_type=F32)


def _dot_nt(a, b):
    return lax.dot_general(a, b, (((1,), (1,)), ((), ())), preferred_element_type=F32)


def _const_spec(shape):
    nd = len(shape)
    return pl.BlockSpec(shape, lambda *_: (0,) * nd, pipeline_mode=pl.Buffered(1))


def _mlp_tile(x, y, wpre_ref, g_ref, wup_ref, wdn_ref, gfin_ref, *, ff_chunk, final):
    x1 = x + _dot(y, wpre_ref[...])
    hn = _rms(x1, g_ref[...]).astype(BF16)
    acc = x1
    d_ff = wup_ref.shape[1]
    for c in range(d_ff // ff_chunk):
        h = _dot(hn, wup_ref[:, c * ff_chunk:(c + 1) * ff_chunk])
        h = jnp.maximum(h, 0.0)
        h = (h * h).astype(BF16)
        acc = acc + _dot(h, wdn_ref[c * ff_chunk:(c + 1) * ff_chunk, :])
    if final:
        acc = _rms(acc, gfin_ref[...])
    return acc


def _mlp_body(x_ref, y_ref, wpre_ref, g_ref, wup_ref, wdn_ref, gfin_ref, o_ref, *, ff_chunk, final):
    o_ref[...] = _mlp_tile(x_ref[...], y_ref[...], wpre_ref, g_ref, wup_ref, wdn_ref, gfin_ref,
                           ff_chunk=ff_chunk, final=final)


def _mlp_layer(x2d, y2d, w_pre, g, w_up, w_dn, g_fin, *, final, tm=512, ff_chunk=512):
    n, d = x2d.shape
    d_ff = w_up.shape[1]
    tm = min(tm, n)
    return pl.pallas_call(
        functools.partial(_mlp_body, ff_chunk=min(ff_chunk, d_ff), final=final),
        grid=(n // tm,),
        in_specs=[
            pl.BlockSpec((tm, d), lambda i: (i, 0)),
            pl.BlockSpec((tm, d), lambda i: (i, 0)),
            _const_spec((d, d)),
            _const_spec((1, d)),
            _const_spec((d, d_ff)),
            _const_spec((d_ff, d)),
            _const_spec((1, d)),
        ],
        out_specs=pl.BlockSpec((tm, d), lambda i: (i, 0)),
        out_shape=jax.ShapeDtypeStruct((n, d), F32),
        compiler_params=pltpu.CompilerParams(
            dimension_semantics=("arbitrary",), vmem_limit_bytes=VMEM_LIMIT_BYTES),
        name="mlp_block",
    )(x2d, y2d, w_pre, g, w_up, w_dn, g_fin)


def _gelu_tanh(x):
    c = math.sqrt(2.0 / math.pi)
    return 0.5 * x * (1.0 + jnp.tanh(c * (x + 0.044715 * (x * x * x))))


def _sigmoid(x):
    return 1.0 / (1.0 + jnp.exp(-x))


def _rec_block(n, r0, rh, yg, yx, cw_ref, cb_ref, wa_ref, ba_ref, wx_ref, bx_ref, lam_ref,
               xbuf, hcar, y_out, *, ts):
    bw = yg.shape[-1]
    cs = slice(n * bw, (n + 1) * bw)
    rs = slice(r0, r0 + rh)
    gate = _gelu_tanh(yg[rs])
    xr = yx[rs]

    xbuf[SUBLANES + r0:SUBLANES + r0 + rh, cs] = xr
    cw = cw_ref[:, cs]
    xc = cb_ref[:, cs] + xbuf[pl.ds(SUBLANES + r0 - 3, rh), cs] * cw[0:1]
    xc = xc + xbuf[pl.ds(SUBLANES + r0 - 2, rh), cs] * cw[1:2]
    xc = xc + xbuf[pl.ds(SUBLANES + r0 - 1, rh), cs] * cw[2:3]
    xc = xc + xr * cw[3:4]
    if r0 + rh == ts:
        xbuf[0:SUBLANES, cs] = xbuf[ts:ts + SUBLANES, cs]

    xcb = xc.astype(BF16)
    r = _sigmoid(_dot(xcb, wa_ref[n]) + ba_ref[:, cs])
    ig = _sigmoid(_dot(xcb, wx_ref[n]) + bx_ref[:, cs])
    nl = -lam_ref[:, cs]
    softplus = jnp.maximum(nl, 0.0) + jnp.log1p(jnp.exp(-jnp.abs(nl)))
    log_a = (-RG_C * r) * softplus
    a = jnp.exp(log_a)
    v = -jnp.tanh(log_a) * (a * a + 1.0)
    root = jnp.where(v > 0.0, v * lax.rsqrt(v), 0.0)
    u = root * (ig * xc)

    row = lax.broadcasted_iota(jnp.int32, (SUBLANES, bw), 0)
    masks = [(s, row >= s) for s in (1, 2, 4)]
    hp = hcar[:, cs]
    hs = []
    for c in range(rh // SUBLANES):
        av = a[c * SUBLANES:(c + 1) * SUBLANES]
        uv = u[c * SUBLANES:(c + 1) * SUBLANES]
        for s, m in masks:
            a_sh = jnp.where(m, pltpu.roll(av, s, 0), 1.0)
            u_sh = jnp.where(m, pltpu.roll(uv, s, 0), 0.0)
            uv = av * u_sh + uv
            av = av * a_sh
        hv = av * hp + uv
        hs.append(hv)
        hp = hv[SUBLANES - 1:SUBLANES, :]
    hcar[:, cs] = hp
    y_out[rs, cs] = (jnp.concatenate(hs, axis=0) * gate).astype(BF16)


def _rec_mlp_body(x_ref, xp_ref, g_ref, win_ref, cw_ref, cb_ref, wa_ref, ba_ref, wx_ref, bx_ref,
                  lam_ref, wpre_ref, gm_ref, wup_ref, wdn_ref, gfin_ref, o_ref,
                  xbuf, hcar, y_s, *, ts, tiles_per_seq, ff_chunk, final):
    t = pl.program_id(0)
    d = x_ref.shape[-1]
    d_ff = wup_ref.shape[1]
    n_chunks = d_ff // ff_chunk
    assert n_chunks % RG_BLOCKS == 0

    @pl.when(lax.rem(t, tiles_per_seq) == 0)
    def _():
        xbuf[0:SUBLANES, :] = jnp.zeros((SUBLANES, d), F32)
        hcar[...] = jnp.zeros((1, d), F32)

    @pl.when(t == 0)
    def _():
        y_s[...] = jnp.zeros(y_s.shape, BF16)

    y2 = _dot(_rms(x_ref[...], g_ref[...]).astype(BF16), win_ref[...])
    bw = d // RG_BLOCKS
    x1 = xp_ref[...] + _dot(y_s[...], wpre_ref[...])
    hn = _rms(x1, gm_ref[...]).astype(BF16)
    acc = x1
    per_block = n_chunks // RG_BLOCKS
    rh = ts // per_block

    def up(c):
        h = jnp.maximum(_dot(hn, wup_ref[:, c * ff_chunk:(c + 1) * ff_chunk]), 0.0)
        return (h * h).astype(BF16)

    h = up(0)
    for n in range(RG_BLOCKS):
        yg = y2[:, n * bw:(n + 1) * bw]
        yx = y2[:, d + n * bw:d + (n + 1) * bw]
        for k in range(per_block):
            c = n * per_block + k
            _rec_block(n, k * rh, rh, yg, yx, cw_ref, cb_ref, wa_ref, ba_ref, wx_ref, bx_ref,
                       lam_ref, xbuf, hcar, y_s, ts=ts)
            h_next = up(c + 1) if c + 1 < n_chunks else None
            acc = acc + _dot(h, wdn_ref[c * ff_chunk:(c + 1) * ff_chunk, :])
            h = h_next
    if final:
        acc = _rms(acc, gfin_ref[...])
    o_ref[...] = acc


def _rec_mlp_layer(x, g, w_in, conv_w, conv_b, w_a, b_a, w_x, b_x, lam,
                   w_pre, g_mlp, w_up, w_dn, g_fin, *, final, ts=256, ff_chunk=1024):
    b, s, d = x.shape
    ts = min(ts, s)
    bw = d // RG_BLOCKS
    d_ff = w_up.shape[1]
    n_tiles = b * s // ts
    x2d = x.reshape(b * s, d)
    cur = lambda t: (jnp.minimum(t, n_tiles - 1), 0)
    prev = lambda t: (jnp.maximum(t - 1, 0), 0)
    out = pl.pallas_call(
        functools.partial(_rec_mlp_body, ts=ts, tiles_per_seq=s // ts,
                          ff_chunk=min(ff_chunk, d_ff), final=final),
        grid=(n_tiles + 1,),
        in_specs=[
            pl.BlockSpec((ts, d), cur),
            pl.BlockSpec((ts, d), prev),
            _const_spec((1, d)),
            _const_spec((d, 2 * d)),
            _const_spec((CONV_W, d)),
            _const_spec((1, d)),
            _const_spec((RG_BLOCKS, bw, bw)),
            _const_spec((1, d)),
            _const_spec((RG_BLOCKS, bw, bw)),
            _const_spec((1, d)),
            _const_spec((1, d)),
            _const_spec((d, d)),
            _const_spec((1, d)),
            _const_spec((d, d_ff)),
            _const_spec((d_ff, d)),
            _const_spec((1, d)),
        ],
        out_specs=pl.BlockSpec((ts, d), prev),
        out_shape=jax.ShapeDtypeStruct((b * s, d), F32),
        scratch_shapes=[
            pltpu.VMEM((ts + SUBLANES, d), F32),
            pltpu.VMEM((1, d), F32),
            pltpu.VMEM((ts, d), BF16),
        ],
        compiler_params=pltpu.CompilerParams(
            dimension_semantics=("arbitrary",), vmem_limit_bytes=VMEM_LIMIT_BYTES,
        ),
        name="rglru_mlp",
    )(x2d, x2d, g, w_in, conv_w, conv_b, w_a, b_a, w_x, b_x, lam, w_pre, g_mlp, w_up, w_dn, g_fin)
    return out.reshape(b, s, d)


def _attproj_body(x_ref, g_ref, w_ref, gkv_ref, wuk_ref,
                  qlat_ref, caug_ref, qi_ref, kia_ref, kib_ref, wi_ref):
    dq = H_ATT * D_QK
    di = H_IDX * D_IDX
    hn = _rms(x_ref[0], g_ref[...]).astype(BF16)
    y = _dot(hn, w_ref[...])
    ts = y.shape[0]
    o = dq
    craw = y[:, o:o + D_LAT]
    o += D_LAT
    qi_ref[0] = y[:, o:o + di].astype(BF16)
    o += di
    kia_ref[0] = y[:, o:o + LANES].astype(BF16)
    o += LANES
    kib_ref[0] = y[:, o:o + LANES].astype(BF16)
    o += LANES
    wi_ref[0] = y[:, o:o + LANES] * IDX_W_SCALE

    c = _rms(craw, gkv_ref[...])
    caug_ref[0] = jnp.concatenate([c, jnp.ones((ts, D_LAT), F32)], axis=1).astype(BF16)

    for m in range(H_ATT // 2):
        qp = y[:, m * LANES:(m + 1) * LANES].astype(BF16)
        ql = _dot(qp, wuk_ref[m]) * (ATT_SCALE * LOG2E)
        qlat_ref[0, 2 * m] = ql[:, :D_LAT].astype(BF16)
        qlat_ref[0, 2 * m + 1] = ql[:, D_LAT:].astype(BF16)


def _att_proj(x, g, w_pack, g_kv, wuk2, *, ts=256):
    b, s, d = x.shape
    ts = min(ts, s)
    n_out = w_pack.shape[1]
    di = H_IDX * D_IDX
    row = lambda i, j: (i, j, 0)
    return pl.pallas_call(
        _attproj_body,
        grid=(b, s // ts),
        in_specs=[
            pl.BlockSpec((1, ts, d), row),
            _const_spec((1, d)),
            _const_spec((d, n_out)),
            _const_spec((1, D_LAT)),
            _const_spec((H_ATT // 2, 2 * D_QK, 2 * D_LAT)),
        ],
        out_specs=[
            pl.BlockSpec((1, H_ATT, ts, D_LAT), lambda i, j: (i, 0, j, 0)),
            pl.BlockSpec((1, ts, 2 * D_LAT), row),
            pl.BlockSpec((1, ts, di), row),
            pl.BlockSpec((1, ts, LANES), row),
            pl.BlockSpec((1, ts, LANES), row),
            pl.BlockSpec((1, ts, LANES), row),
        ],
        out_shape=[
            jax.ShapeDtypeStruct((b, H_ATT, s, D_LAT), BF16),
            jax.ShapeDtypeStruct((b, s, 2 * D_LAT), BF16),
            jax.ShapeDtypeStruct((b, s, di), BF16),
            jax.ShapeDtypeStruct((b, s, LANES), BF16),
            jax.ShapeDtypeStruct((b, s, LANES), BF16),
            jax.ShapeDtypeStruct((b, s, LANES), F32),
        ],
        compiler_params=pltpu.CompilerParams(
            dimension_semantics=("arbitrary", "arbitrary"), vmem_limit_bytes=VMEM_LIMIT_BYTES),
        name="att_proj",
    )(x, g, w_pack, g_kv, wuk2)


def _t5_bucket(dist):
    n = jnp.maximum(dist, 0)
    nf = jnp.maximum(n, 1).astype(F32)
    large = MAX_EXACT + (jnp.log(nf / MAX_EXACT) / math.log(MAX_DISTANCE / MAX_EXACT)
                         * (NUM_BUCKETS - MAX_EXACT)).astype(jnp.int32)
    large = jnp.minimum(large, NUM_BUCKETS - 1)
    return jnp.where(n < MAX_EXACT, n, large)


def _attn_body(rb_ref, qlat_ref, qi_ref, wi_ref, caug_ref, kia_ref, kib_ref, wuv_ref, o_ref,
               sc_std, sc_t, acc, mrun, alph, btab, wib, p_s, tj, thr, jthr, *, topk, seq):
    tq, tk = Q_BLOCK, K_TILE
    nb = tk // tq
    step = pl.program_id(1)
    n_kt = step + 1
    hq = H_ATT * tq
    pairs = H_IDX // 2

    @pl.when((pl.program_id(0) == 0) & (step == 0))
    def _():
        ql = lax.broadcasted_iota(jnp.int32, (tq, tq), 0)
        kl = lax.broadcasted_iota(jnp.int32, (tq, tq), 1)
        for var in range(2):
            bucket = _t5_bucket(ql - kl + var * tq)
            for h in range(H_ATT):
                t = jnp.zeros((tq, tq), F32)
                for bk in range(NUM_BUCKETS - 1):
                    t = jnp.where(bucket == bk, rb_ref[bk, h] - rb_ref[NUM_BUCKETS - 1, h], t)
                btab[h, var] = t * LOG2E

    qi = qi_ref[0]
    qi8 = jnp.concatenate(
        [qi[blk * tq:(blk + 1) * tq, m * LANES:(m + 1) * LANES]
         for blk in range(nb) for m in range(pairs)], axis=0)
    wi = wi_ref[0]
    for blk in range(nb):
        for h in range(H_IDX):
            wib[blk, h] = jnp.broadcast_to(wi[blk * tq:(blk + 1) * tq, h:h + 1], (tq, tk))
    row_q = lax.broadcasted_iota(jnp.int32, (tq, tk), 0)
    lane_k = lax.broadcasted_iota(jnp.int32, (tq, tk), 1)

    def score_tile(kt, carry, *, last):
        k0 = pl.multiple_of(kt * tk, tk)
        da = jnp.maximum(_dot_nt(qi8, kia_ref[0, pl.ds(k0, tk), :]), 0.0)
        db = jnp.maximum(_dot_nt(qi8, kib_ref[0, pl.ds(k0, tk), :]), 0.0)
        for blk in range(nb):
            s = jnp.zeros((tq, tk), F32)
            for m in range(pairs):
                r0 = (blk * pairs + m) * tq
                s = s + wib[blk, 2 * m] * da[r0:r0 + tq]
                s = s + wib[blk, 2 * m + 1] * db[r0:r0 + tq]
            if last:
                s = jnp.where(lane_k <= row_q + blk * tq, s, NEG_INF)
            sc_std[blk, kt] = s
            sc_t[kt, :, blk * tq:(blk + 1) * tq] = s.T
        return carry

    def score_pair(j, carry):
        score_tile(2 * j, carry, last=False)
        return score_tile(2 * j + 1, carry, last=False)

    lax.fori_loop(0, lax.shift_right_logical(step, 1), score_pair, 0)

    @pl.when((step & 1) == 1)
    def _():
        score_tile(step - 1, 0, last=False)

    score_tile(step, 0, last=True)

    st = (SUBLANES, nb * tq)
    p_lane = step * (nb * tq) + lax.broadcasted_iota(jnp.int32, st, 1)
    kq = jnp.minimum(topk, p_lane + 1).astype(F32)

    def search(n):
        def over_tiles(fn, op):
            pair = {jnp.sum: jnp.add, jnp.max: jnp.maximum, jnp.min: jnp.minimum}[op]
            parts = [op(fn(sc_t[kt].reshape(tk // SUBLANES, SUBLANES, nb * tq), kt), axis=0)
                     for kt in range(n)]
            while len(parts) > 1:
                parts = [pair(*parts[i:i + 2]) if i + 1 < len(parts) else parts[i]
                         for i in range(0, len(parts), 2)]
            x = parts[0]
            for s in (4, 2, 1):
                x = pair(x, pltpu.roll(x, s, 0))
            return x

        def count_ge(th):
            return over_tiles(lambda v, kt: jnp.where(v >= th, 1.0, 0.0), jnp.sum)

        mx = over_tiles(lambda v, kt: v, jnp.max)
        lo0 = over_tiles(lambda v, kt: jnp.where(v == NEG_INF, POS_INF, v), jnp.min)

        def bisect(_, c):
            lo, hix, chi = c
            hib = jnp.where(hix == POS_INF, mx, hix)
            mid = 0.5 * lo + 0.5 * hib
            cnt = count_ge(mid)
            ge = cnt >= kq
            return jnp.where(ge, mid, lo), jnp.where(ge, hix, mid), jnp.where(ge, chi, cnt)

        _, hix, chi = lax.fori_loop(
            0, N_BISECT, bisect,
            (lo0, jnp.full(st, POS_INF, F32), jnp.zeros(st, F32)))

        def sweep(cand):
            cs, ms = [], []
            for kt in range(n):
                v = sc_t[kt].reshape(tk // SUBLANES, SUBLANES, nb * tq)
                ge = v >= cand
                cs.append(jnp.sum(jnp.where(ge, 1.0, 0.0), axis=0))
                ms.append(jnp.max(jnp.where(ge, NEG_INF, v), axis=0))
            c, m = functools.reduce(jnp.add, cs), functools.reduce(jnp.maximum, ms)
            for s in (4, 2, 1):
                c = c + pltpu.roll(c, s, 0)
                m = jnp.maximum(m, pltpu.roll(m, s, 0))
            return c, m

        def snap_cond(c):
            return jnp.max(c[5]) > 0.0

        def snap_body(c):
            cand, chi, t, cge, ngt, pending = c
            ct, nxt = sweep(cand)
            act = pending > 0.0
            ok = ct >= kq
            t = jnp.where(act, cand, t)
            cge = jnp.where(act, ct, cge)
            ngt = jnp.where(act, chi, ngt)
            pending = jnp.where(act & ok, 0.0, pending)
            cand = jnp.where(act & ~ok, nxt, cand)
            chi = jnp.where(act & ~ok, ct, chi)
            return cand, chi, t, cge, ngt, pending

        cand0 = over_tiles(lambda v, kt: jnp.where(v < hix, v, NEG_INF), jnp.max)
        zero = jnp.zeros(st, F32)
        _, _, t_k, cge, ngt, _ = lax.while_loop(
            snap_cond, snap_body, (cand0, chi, zero, zero, zero, jnp.ones(st, F32)))
        thr[...] = t_k
        jthr[...] = jnp.full(st, float(seq), F32)

        need = kq - ngt
        excess = (cge - ngt) > need

        @pl.when(jnp.max(jnp.where(excess, 1.0, 0.0)) > 0.0)
        def _():
            sub_k = lax.broadcasted_iota(jnp.int32, (tk, nb * tq), 0).astype(F32).reshape(
                tk // SUBLANES, SUBLANES, nb * tq)

            def jbisect(_, c):
                jlo, jhi = c
                mid = jnp.floor(0.5 * (jlo + jhi))
                cnt = over_tiles(
                    lambda v, kt: jnp.where((v == t_k) & (sub_k + float(kt * tk) <= mid), 1.0, 0.0),
                    jnp.sum)
                ge = cnt >= need
                return jnp.where(ge, jlo, mid), jnp.where(ge, mid, jhi)

            n_pass = max(1, int(math.ceil(math.log2(seq))))
            _, jhi = lax.fori_loop(
                0, n_pass, jbisect,
                (jnp.full(st, -1.0, F32), jnp.full(st, float(seq - 1), F32)))
            jthr[...] = jhi

    for n in range(1, seq // tk + 1):
        pl.when(n_kt == n)(functools.partial(search, n))

    for blk in range(nb):
        for i, ref in enumerate((thr, jthr)):
            v = jnp.concatenate([ref[:, blk * tq:(blk + 1) * tq]] * (tq // SUBLANES), axis=0).T
            tj[blk, i] = jnp.concatenate([v] * (tk // tq), axis=1)

    lane_kf = lane_k.astype(F32)

    def attend_tile(kt, carry=None, *, kind, first=False):
        qall = jnp.concatenate(
            [qlat_ref[0, :, blk * tq:(blk + 1) * tq, :].reshape(hq, D_LAT) for blk in range(nb)],
            axis=0)
        k0 = pl.multiple_of(kt * tk, tk)
        ct = caug_ref[0, pl.ds(k0, tk), :]
        src = _dot_nt(qall, ct[:, :D_LAT])
        kf = lane_kf + (kt * tk).astype(F32)
        for blk in range(nb):
            s = sc_std[blk, kt]
            tb = tj[blk, 0]
            sel = (s > tb) | ((s == tb) & (kf <= tj[blk, 1]))
            ma = jnp.where(sel, 0.0, NEG_INF)
            for h in range(H_ATT):
                r0 = (blk * H_ATT + h) * tq
                lh = src[r0:r0 + tq, :] + ma
                if kind == "near" and blk == 0:
                    lh = lh + jnp.concatenate([jnp.zeros((tq, tq), F32), btab[h, 1]], axis=1)
                elif kind == "last" and blk == 0:
                    lh = lh + jnp.concatenate([btab[h, 0], jnp.zeros((tq, tq), F32)], axis=1)
                elif kind == "last" and blk == 1:
                    lh = lh + jnp.concatenate([btab[h, 1], btab[h, 0]], axis=1)
                mt = jnp.maximum(lh[:, :tq], lh[:, tq:])
                rm = jnp.broadcast_to(jnp.max(mt, axis=1, keepdims=True), (tq, tq))
                if first:
                    m_new = jnp.maximum(rm, M_INIT)
                else:
                    m_old = mrun[blk, h]
                    m_new = jnp.maximum(m_old, rm)
                    alph[blk, h] = jnp.exp2(m_old - m_new)
                mrun[blk, h] = m_new
                m2 = jnp.concatenate([m_new, m_new], axis=1)
                p_s[r0:r0 + tq, :] = jnp.exp2(lh - m2).astype(BF16)
        pv = _dot(p_s[...], ct)
        if first:
            acc[...] = pv
            return carry
        for blk in range(nb):
            for h in range(H_ATT):
                r0 = (blk * H_ATT + h) * tq
                al = alph[blk, h]
                acc[r0:r0 + tq, :] = acc[r0:r0 + tq, :] * jnp.concatenate([al, al], axis=1) + pv[r0:r0 + tq]
        return carry

    attend_tile(step, kind="last", first=True)

    @pl.when(step >= 1)
    def _():
        attend_tile(step - 1, kind="near")

    lax.fori_loop(0, jnp.maximum(step - 1, 0), functools.partial(attend_tile, kind="far"), 0)

    for m in range(H_ATT // 2):
        rows = []
        for blk in range(nb):
            halves = []
            for h in (2 * m, 2 * m + 1):
                r0 = (blk * H_ATT + h) * tq
                a = acc[r0:r0 + tq, :]
                halves.append(a[:, :D_LAT] * (1.0 / a[:, D_LAT:]))
            rows.append(jnp.concatenate(halves, axis=1))
        pair = jnp.concatenate(rows, axis=0).astype(BF16)
        o_ref[0, :, m * 2 * D_V:(m + 1) * 2 * D_V] = _dot(pair, wuv_ref[m]).astype(BF16)


def _attention(rel_bias, qlat, qi, wi, caug, kia, kib, wuv2, *, topk):
    b, _, s, _ = qlat.shape
    tq, tk = Q_BLOCK, K_TILE
    assert s % tk == 0 and tk == 2 * tq
    nb = tk // tq
    n_kt = s // tk
    hq = H_ATT * tq
    di = H_IDX * D_IDX
    dv = H_ATT * D_V
    qrow = lambda i, j: (i, j, 0)
    full = lambda i, j: (i, 0, 0)
    return pl.pallas_call(
        functools.partial(_attn_body, topk=topk, seq=s),
        grid=(b, s // tk),
        in_specs=[
            pl.BlockSpec(memory_space=pltpu.SMEM),
            pl.BlockSpec((1, H_ATT, tk, D_LAT), lambda i, j: (i, 0, j, 0)),
            pl.BlockSpec((1, tk, di), qrow),
            pl.BlockSpec((1, tk, LANES), qrow),
            pl.BlockSpec((1, s, 2 * D_LAT), full),
            pl.BlockSpec((1, s, LANES), full),
            pl.BlockSpec((1, s, LANES), full),
            _const_spec((H_ATT // 2, 2 * D_LAT, 2 * D_V)),
        ],
        out_specs=pl.BlockSpec((1, tk, dv), qrow),
        out_shape=jax.ShapeDtypeStruct((b, s, dv), BF16),
        scratch_shapes=[
            pltpu.VMEM((nb, n_kt, tq, tk), F32),
            pltpu.VMEM((n_kt, tk, nb * tq), F32),
            pltpu.VMEM((nb * hq, tk), F32),
            pltpu.VMEM((nb, H_ATT, tq, tq), F32),
            pltpu.VMEM((nb, H_ATT, tq, tq), F32),
            pltpu.VMEM((H_ATT, 2, tq, tq), F32),
            pltpu.VMEM((nb, H_IDX, tq, tk), F32),
            pltpu.VMEM((nb * hq, tk), BF16),
            pltpu.VMEM((nb, 2, tq, tk), F32),
            pltpu.VMEM((SUBLANES, nb * tq), F32),
            pltpu.VMEM((SUBLANES, nb * tq), F32),
        ],
        compiler_params=pltpu.CompilerParams(
            dimension_semantics=("arbitrary", "arbitrary"), vmem_limit_bytes=VMEM_LIMIT_BYTES),
        name="dsa_attention",
    )(rel_bias, qlat, qi, wi, caug, kia, kib, wuv2)


def _block_diag_pairs(w):
    h, a, b = w.shape
    w = w.reshape(h // 2, 2, a, b)
    z = jnp.zeros((h // 2, a, b), w.dtype)
    top = jnp.concatenate([w[:, 0], z], axis=2)
    bot = jnp.concatenate([z, w[:, 1]], axis=2)
    return jnp.concatenate([top, bot], axis=1)


def _pack_att_w(w_in):
    dq = H_ATT * D_QK
    di = H_IDX * D_IDX
    d = w_in.shape[0]
    o = dq + D_LAT + di
    w_ki = w_in[:, o:o + D_IDX]
    w_wi = w_in[:, o + D_IDX:o + D_IDX + H_IDX]
    z_ki = jnp.zeros((d, LANES - D_IDX), w_in.dtype)
    z_wi = jnp.zeros((d, LANES - H_IDX), w_in.dtype)
    return jnp.concatenate([w_in[:, :o], w_ki, z_ki, z_ki, w_ki, w_wi, z_wi], axis=1)


def kernel(x, norm_mix_g, norm_mlp_g, final_norm_g, rec_w_in, rec_conv_w, rec_conv_b, rec_w_a, rec_b_a, rec_w_x, rec_b_x, rec_lambda, rec_w_out, att_w_in, att_kv_norm_g, att_w_uk, att_w_uv, att_w_o, rel_bias, mlp_w_up, mlp_w_down):
    b, s, d = x.shape
    depth = norm_mix_g.shape[0]
    topk = min(TOPK_MAX, s // 4)
    row = lambda v: v.reshape(1, -1).astype(F32)
    for layer in range(depth):
        j = layer // 2
        g_mix = row(norm_mix_g[layer])
        final = layer == depth - 1
        mlp_w = (row(norm_mlp_g[layer]), mlp_w_up[layer].astype(BF16), mlp_w_down[layer].astype(BF16),
                 row(final_norm_g))
        if layer % 2 == 0:
            x = _rec_mlp_layer(
                x, g_mix, rec_w_in[j].astype(BF16), rec_conv_w[j].astype(F32), row(rec_conv_b[j]),
                rec_w_a[j].astype(BF16), row(rec_b_a[j]), rec_w_x[j].astype(BF16), row(rec_b_x[j]),
                row(rec_lambda[j]), rec_w_out[j].astype(BF16), *mlp_w, final=final)
        else:
            qlat, caug, qi, kia, kib, wi = _att_proj(
                x, g_mix, _pack_att_w(att_w_in[j]).astype(BF16), row(att_kv_norm_g[j]),
                _block_diag_pairs(att_w_uk[j]).astype(BF16))
            y = _attention(
                rel_bias.astype(F32), qlat, qi, wi, caug, kia, kib,
                _block_diag_pairs(att_w_uv[j]).astype(BF16), topk=topk)
            x = _mlp_layer(
                x.reshape(b * s, d), y.reshape(b * s, d), att_w_o[j].astype(BF16), *mlp_w,
                final=final).reshape(b, s, d)
    return x
```

```python
import functools
import math

import jax
import jax.numpy as jnp
from jax import lax
from jax.experimental import pallas as pl
from jax.experimental.pallas import tpu as pltpu

F32 = jnp.float32
BF16 = jnp.bfloat16

EPS = 1e-6
RG_BLOCKS = 4
CONV_W = 4
RG_C = 8.0
H_ATT = 16
D_QK = 64
D_V = 64
D_LAT = 128
H_IDX = 8
D_IDX = 64
TOPK_MAX = 256
ATT_SCALE = D_QK ** -0.5
LOG2E = math.log2(math.e)
IDX_W_SCALE = (H_IDX ** -0.5) * (D_IDX ** -0.5)
NUM_BUCKETS = 32
MAX_DISTANCE = 128
MAX_EXACT = NUM_BUCKETS // 2

LANES = 128
SUBLANES = 8
VMEM_LIMIT_BYTES = 56 * 1024 * 1024

Q_BLOCK = 128
K_TILE = 256
N_BISECT = 14
NEG_INF = float("-inf")
POS_INF = float("inf")
M_INIT = -1e30


def _rms(x, g):
    ms = jnp.mean(x * x, axis=-1, keepdims=True)
    return x * lax.rsqrt(ms + EPS) * g


def _dot(a, b):
    return jnp.dot(a, b, preferred_element_type=F32)


def _dot_nt(a, b):
    return lax.dot_general(a, b, (((1,), (1,)), ((), ())), preferred_element_type=F32)


def _const_spec(shape):
    nd = len(shape)
    return pl.BlockSpec(shape, lambda *_: (0,) * nd, pipeline_mode=pl.Buffered(1))


def _mlp_tile(x, y, wpre_ref, g_ref, wup_ref, wdn_ref, gfin_ref, *, ff_chunk, final):
    x1 = x + _dot(y, wpre_ref[...])
    hn = _rms(x1, g_ref[...]).astype(BF16)
    acc = x1
    d_ff = wup_ref.shape[1]
    for c in range(d_ff // ff_chunk):
        h = _dot(hn, wup_ref[:, c * ff_chunk:(c + 1) * ff_chunk])
        h = jnp.maximum(h, 0.0)
        h = (h * h).astype(BF16)
        acc = acc + _dot(h, wdn_ref[c * ff_chunk:(c + 1) * ff_chunk, :])
    if final:
        acc = _rms(acc, gfin_ref[...])
    return acc


def _mlp_body(x_ref, y_ref, wpre_ref, g_ref, wup_ref, wdn_ref, gfin_ref, o_ref, *, ff_chunk, final):
    o_ref[...] = _mlp_tile(x_ref[...], y_ref[...], wpre_ref, g_ref, wup_ref, wdn_ref, gfin_ref,
                           ff_chunk=ff_chunk, final=final)


def _mlp_layer(x2d, y2d, w_pre, g, w_up, w_dn, g_fin, *, final, tm=512, ff_chunk=512):
    n, d = x2d.shape
    d_ff = w_up.shape[1]
    tm = min(tm, n)
    return pl.pallas_call(
        functools.partial(_mlp_body, ff_chunk=min(ff_chunk, d_ff), final=final),
        grid=(n // tm,),
        in_specs=[
            pl.BlockSpec((tm, d), lambda i: (i, 0)),
            pl.BlockSpec((tm, d), lambda i: (i, 0)),
            _const_spec((d, d)),
            _const_spec((1, d)),
            _const_spec((d, d_ff)),
            _const_spec((d_ff, d)),
            _const_spec((1, d)),
        ],
        out_specs=pl.BlockSpec((tm, d), lambda i: (i, 0)),
        out_shape=jax.ShapeDtypeStruct((n, d), F32),
        compiler_params=pltpu.CompilerParams(
            dimension_semantics=("arbitrary",), vmem_limit_bytes=VMEM_LIMIT_BYTES),
        name="mlp_block",
    )(x2d, y2d, w_pre, g, w_up, w_dn, g_fin)


def _gelu_tanh(x):
    c = math.sqrt(2.0 / math.pi)
    return 0.5 * x * (1.0 + jnp.tanh(c * (x + 0.044715 * (x * x * x))))


def _sigmoid(x):
    return 1.0 / (1.0 + jnp.exp(-x))


def _rec_block(n, yg, xr, cw_ref, cb_ref, wa_ref, ba_ref, wx_ref, bx_ref, lam_ref,
               xbuf, hcar, y_out):
    ts, bw = yg.shape
    cs = slice(n * bw, (n + 1) * bw)
    gate = _gelu_tanh(yg)

    xbuf[SUBLANES:SUBLANES + ts, cs] = xr
    cw = cw_ref[:, cs]
    xc = cb_ref[:, cs] + xbuf[pl.ds(SUBLANES - 3, ts), cs] * cw[0:1]
    xc = xc + xbuf[pl.ds(SUBLANES - 2, ts), cs] * cw[1:2]
    xc = xc + xbuf[pl.ds(SUBLANES - 1, ts), cs] * cw[2:3]
    xc = xc + xr * cw[3:4]
    xbuf[0:SUBLANES, cs] = xbuf[ts:ts + SUBLANES, cs]

    xcb = xc.astype(BF16)
    r = _sigmoid(_dot(xcb, wa_ref[n]) + ba_ref[:, cs])
    ig = _sigmoid(_dot(xcb, wx_ref[n]) + bx_ref[:, cs])
    nl = -lam_ref[:, cs]
    softplus = jnp.maximum(nl, 0.0) + jnp.log1p(jnp.exp(-jnp.abs(nl)))
    log_a = (-RG_C * r) * softplus
    a = jnp.exp(log_a)
    v = -jnp.tanh(log_a) * (a * a + 1.0)
    root = jnp.where(v > 0.0, v * lax.rsqrt(v), 0.0)
    u = root * (ig * xc)

    row = lax.broadcasted_iota(jnp.int32, (SUBLANES, bw), 0)
    masks = [(s, row >= s) for s in (1, 2, 4)]
    hp = hcar[:, cs]
    hs = []
    for c in range(ts // SUBLANES):
        av = a[c * SUBLANES:(c + 1) * SUBLANES]
        uv = u[c * SUBLANES:(c + 1) * SUBLANES]
        for s, m in masks:
            a_sh = jnp.where(m, pltpu.roll(av, s, 0), 1.0)
            u_sh = jnp.where(m, pltpu.roll(uv, s, 0), 0.0)
            uv = av * u_sh + uv
            av = av * a_sh
        hv = av * hp + uv
        hs.append(hv)
        hp = hv[SUBLANES - 1:SUBLANES, :]
    hcar[:, cs] = hp
    y_out[:, cs] = (jnp.concatenate(hs, axis=0) * gate).astype(BF16)


def _rec_mlp_body(x_ref, xp_ref, g_ref, win_ref, cw_ref, cb_ref, wa_ref, ba_ref, wx_ref, bx_ref,
                  lam_ref, wpre_ref, gm_ref, wup_ref, wdn_ref, gfin_ref, o_ref,
                  xbuf, hcar, y_s, *, ts, tiles_per_seq, ff_chunk, final):
    t = pl.program_id(0)
    d = x_ref.shape[-1]
    d_ff = wup_ref.shape[1]
    n_chunks = d_ff // ff_chunk

    @pl.when(lax.rem(t, tiles_per_seq) == 0)
    def _():
        xbuf[0:SUBLANES, :] = jnp.zeros((SUBLANES, d), F32)
        hcar[...] = jnp.zeros((1, d), F32)

    @pl.when(t == 0)
    def _():
        y_s[...] = jnp.zeros(y_s.shape, BF16)

    y2 = _dot(_rms(x_ref[...], g_ref[...]).astype(BF16), win_ref[...])
    bw = d // RG_BLOCKS
    x1 = xp_ref[...] + _dot(y_s[...], wpre_ref[...])
    hn = _rms(x1, gm_ref[...]).astype(BF16)
    acc = x1
    hs = {}

    def up(c):
        h = jnp.maximum(_dot(hn, wup_ref[:, c * ff_chunk:(c + 1) * ff_chunk]), 0.0)
        hs[c] = (h * h).astype(BF16)

    def down(c):
        return _dot(hs.pop(c), wdn_ref[c * ff_chunk:(c + 1) * ff_chunk, :])

    mlp_ops = []
    for c in range(n_chunks):
        if c + 1 < n_chunks:
            mlp_ops.append((up, c + 1))
        mlp_ops.append((down, c))
    up(0)
    for n in range(RG_BLOCKS):
        _rec_block(n, y2[:, n * bw:(n + 1) * bw], y2[:, d + n * bw:d + (n + 1) * bw],
                   cw_ref, cb_ref, wa_ref, ba_ref, wx_ref, bx_ref, lam_ref, xbuf, hcar, y_s)
        take = -(-len(mlp_ops) // (RG_BLOCKS - n))
        for fn, c in mlp_ops[:take]:
            r = fn(c)
            if r is not None:
                acc = acc + r
        mlp_ops = mlp_ops[take:]
    if final:
        acc = _rms(acc, gfin_ref[...])
    o_ref[...] = acc


def _rec_mlp_layer(x, g, w_in, conv_w, conv_b, w_a, b_a, w_x, b_x, lam,
                   w_pre, g_mlp, w_up, w_dn, g_fin, *, final, ts=256, ff_chunk=1024):
    b, s, d = x.shape
    ts = min(ts, s)
    bw = d // RG_BLOCKS
    d_ff = w_up.shape[1]
    n_tiles = b * s // ts
    x2d = x.reshape(b * s, d)
    cur = lambda t: (jnp.minimum(t, n_tiles - 1), 0)
    prev = lambda t: (jnp.maximum(t - 1, 0), 0)
    out = pl.pallas_call(
        functools.partial(_rec_mlp_body, ts=ts, tiles_per_seq=s // ts,
                          ff_chunk=min(ff_chunk, d_ff), final=final),
        grid=(n_tiles + 1,),
        in_specs=[
            pl.BlockSpec((ts, d), cur),
            pl.BlockSpec((ts, d), prev),
            _const_spec((1, d)),
            _const_spec((d, 2 * d)),
            _const_spec((CONV_W, d)),
            _const_spec((1, d)),
            _const_spec((RG_BLOCKS, bw, bw)),
            _const_spec((1, d)),
            _const_spec((RG_BLOCKS, bw, bw)),
            _const_spec((1, d)),
            _const_spec((1, d)),
            _const_spec((d, d)),
            _const_spec((1, d)),
            _const_spec((d, d_ff)),
            _const_spec((d_ff, d)),
            _const_spec((1, d)),
        ],
        out_specs=pl.BlockSpec((ts, d), prev),
        out_shape=jax.ShapeDtypeStruct((b * s, d), F32),
        scratch_shapes=[
            pltpu.VMEM((ts + SUBLANES, d), F32),
            pltpu.VMEM((1, d), F32),
            pltpu.VMEM((ts, d), BF16),
        ],
        compiler_params=pltpu.CompilerParams(
            dimension_semantics=("arbitrary",), vmem_limit_bytes=VMEM_LIMIT_BYTES,
        ),
        name="rglru_mlp",
    )(x2d, x2d, g, w_in, conv_w, conv_b, w_a, b_a, w_x, b_x, lam, w_pre, g_mlp, w_up, w_dn, g_fin)
    return out.reshape(b, s, d)


def _attproj_body(x_ref, g_ref, w_ref, gkv_ref, wuk_ref,
                  qlat_ref, caug_ref, qi_ref, kia_ref, kib_ref, wi_ref):
    dq = H_ATT * D_QK
    di = H_IDX * D_IDX
    hn = _rms(x_ref[0], g_ref[...]).astype(BF16)
    y = _dot(hn, w_ref[...])
    ts = y.shape[0]
    o = dq
    craw = y[:, o:o + D_LAT]
    o += D_LAT
    qi_ref[0] = y[:, o:o + di].astype(BF16)
    o += di
    kia_ref[0] = y[:, o:o + LANES].astype(BF16)
    o += LANES
    kib_ref[0] = y[:, o:o + LANES].astype(BF16)
    o += LANES
    wi_ref[0] = y[:, o:o + LANES] * IDX_W_SCALE

    c = _rms(craw, gkv_ref[...])
    caug_ref[0] = jnp.concatenate([c, jnp.ones((ts, D_LAT), F32)], axis=1).astype(BF16)

    for m in range(H_ATT // 2):
        qp = y[:, m * LANES:(m + 1) * LANES].astype(BF16)
        ql = _dot(qp, wuk_ref[m]) * (ATT_SCALE * LOG2E)
        qlat_ref[0, 2 * m] = ql[:, :D_LAT].astype(BF16)
        qlat_ref[0, 2 * m + 1] = ql[:, D_LAT:].astype(BF16)


def _att_proj(x, g, w_pack, g_kv, wuk2, *, ts=256):
    b, s, d = x.shape
    ts = min(ts, s)
    n_out = w_pack.shape[1]
    di = H_IDX * D_IDX
    row = lambda i, j: (i, j, 0)
    return pl.pallas_call(
        _attproj_body,
        grid=(b, s // ts),
        in_specs=[
            pl.BlockSpec((1, ts, d), row),
            _const_spec((1, d)),
            _const_spec((d, n_out)),
            _const_spec((1, D_LAT)),
            _const_spec((H_ATT // 2, 2 * D_QK, 2 * D_LAT)),
        ],
        out_specs=[
            pl.BlockSpec((1, H_ATT, ts, D_LAT), lambda i, j: (i, 0, j, 0)),
            pl.BlockSpec((1, ts, 2 * D_LAT), row),
            pl.BlockSpec((1, ts, di), row),
            pl.BlockSpec((1, ts, LANES), row),
            pl.BlockSpec((1, ts, LANES), row),
            pl.BlockSpec((1, ts, LANES), row),
        ],
        out_shape=[
            jax.ShapeDtypeStruct((b, H_ATT, s, D_LAT), BF16),
            jax.ShapeDtypeStruct((b, s, 2 * D_LAT), BF16),
            jax.ShapeDtypeStruct((b, s, di), BF16),
            jax.ShapeDtypeStruct((b, s, LANES), BF16),
            jax.ShapeDtypeStruct((b, s, LANES), BF16),
            jax.ShapeDtypeStruct((b, s, LANES), F32),
        ],
        compiler_params=pltpu.CompilerParams(
            dimension_semantics=("arbitrary", "arbitrary"), vmem_limit_bytes=VMEM_LIMIT_BYTES),
        name="att_proj",
    )(x, g, w_pack, g_kv, wuk2)


def _t5_bucket(dist):
    n = jnp.maximum(dist, 0)
    nf = jnp.maximum(n, 1).astype(F32)
    large = MAX_EXACT + (jnp.log(nf / MAX_EXACT) / math.log(MAX_DISTANCE / MAX_EXACT)
                         * (NUM_BUCKETS - MAX_EXACT)).astype(jnp.int32)
    large = jnp.minimum(large, NUM_BUCKETS - 1)
    return jnp.where(n < MAX_EXACT, n, large)


def _attn_body(rb_ref, qlat_ref, qi_ref, wi_ref, caug_ref, kia_ref, kib_ref, wuv_ref, o_ref,
               sc_std, sc_t, acc, mrun, alph, btab, wib, p_s, tj, thr, jthr, *, topk, seq):
    tq, tk = Q_BLOCK, K_TILE
    nb = tk // tq
    step = pl.program_id(1)
    n_kt = step + 1
    hq = H_ATT * tq
    pairs = H_IDX // 2

    @pl.when((pl.program_id(0) == 0) & (step == 0))
    def _():
        ql = lax.broadcasted_iota(jnp.int32, (tq, tq), 0)
        kl = lax.broadcasted_iota(jnp.int32, (tq, tq), 1)
        for var in range(2):
            bucket = _t5_bucket(ql - kl + var * tq)
            for h in range(H_ATT):
                t = jnp.zeros((tq, tq), F32)
                for bk in range(NUM_BUCKETS - 1):
                    t = jnp.where(bucket == bk, rb_ref[bk, h] - rb_ref[NUM_BUCKETS - 1, h], t)
                btab[h, var] = t * LOG2E

    qi = qi_ref[0]
    qi8 = jnp.concatenate(
        [qi[blk * tq:(blk + 1) * tq, m * LANES:(m + 1) * LANES]
         for blk in range(nb) for m in range(pairs)], axis=0)
    wi = wi_ref[0]
    for blk in range(nb):
        for h in range(H_IDX):
            wib[blk, h] = jnp.broadcast_to(wi[blk * tq:(blk + 1) * tq, h:h + 1], (tq, tk))
    row_q = lax.broadcasted_iota(jnp.int32, (tq, tk), 0)
    lane_k = lax.broadcasted_iota(jnp.int32, (tq, tk), 1)

    def score_tile(kt, carry, *, last):
        k0 = pl.multiple_of(kt * tk, tk)
        da = jnp.maximum(_dot_nt(qi8, kia_ref[0, pl.ds(k0, tk), :]), 0.0)
        db = jnp.maximum(_dot_nt(qi8, kib_ref[0, pl.ds(k0, tk), :]), 0.0)
        for blk in range(nb):
            s = jnp.zeros((tq, tk), F32)
            for m in range(pairs):
                r0 = (blk * pairs + m) * tq
                s = s + wib[blk, 2 * m] * da[r0:r0 + tq]
                s = s + wib[blk, 2 * m + 1] * db[r0:r0 + tq]
            if last:
                s = jnp.where(lane_k <= row_q + blk * tq, s, NEG_INF)
            sc_std[blk, kt] = s
            sc_t[kt, :, blk * tq:(blk + 1) * tq] = s.T
        return carry

    def score_pair(j, carry):
        score_tile(2 * j, carry, last=False)
        return score_tile(2 * j + 1, carry, last=False)

    lax.fori_loop(0, lax.shift_right_logical(step, 1), score_pair, 0)

    @pl.when((step & 1) == 1)
    def _():
        score_tile(step - 1, 0, last=False)

    score_tile(step, 0, last=True)

    st = (SUBLANES, nb * tq)
    p_lane = step * (nb * tq) + lax.broadcasted_iota(jnp.int32, st, 1)
    kq = jnp.minimum(topk, p_lane + 1).astype(F32)

    def search(n):
        def over_tiles(fn, op):
            pair = {jnp.sum: jnp.add, jnp.max: jnp.maximum, jnp.min: jnp.minimum}[op]
            parts = [op(fn(sc_t[kt].reshape(tk // SUBLANES, SUBLANES, nb * tq), kt), axis=0)
                     for kt in range(n)]
            while len(parts) > 1:
                parts = [pair(*parts[i:i + 2]) if i + 1 < len(parts) else parts[i]
                         for i in range(0, len(parts), 2)]
            x = parts[0]
            for s in (4, 2, 1):
                x = pair(x, pltpu.roll(x, s, 0))
            return x

        def count_ge(th):
            return over_tiles(lambda v, kt: jnp.where(v >= th, 1.0, 0.0), jnp.sum)

        mx = over_tiles(lambda v, kt: v, jnp.max)
        lo0 = over_tiles(lambda v, kt: jnp.where(v == NEG_INF, POS_INF, v), jnp.min)

        def bisect(_, c):
            lo, hix, chi = c
            hib = jnp.where(hix == POS_INF, mx, hix)
            mid = 0.5 * lo + 0.5 * hib
            cnt = count_ge(mid)
            ge = cnt >= kq
            return jnp.where(ge, mid, lo), jnp.where(ge, hix, mid), jnp.where(ge, chi, cnt)

        _, hix, chi = lax.fori_loop(
            0, N_BISECT, bisect,
            (lo0, jnp.full(st, POS_INF, F32), jnp.zeros(st, F32)))

        def sweep(cand):
            cs, ms = [], []
            for kt in range(n):
                v = sc_t[kt].reshape(tk // SUBLANES, SUBLANES, nb * tq)
                ge = v >= cand
                cs.append(jnp.sum(jnp.where(ge, 1.0, 0.0), axis=0))
                ms.append(jnp.max(jnp.where(ge, NEG_INF, v), axis=0))
            c, m = functools.reduce(jnp.add, cs), functools.reduce(jnp.maximum, ms)
            for s in (4, 2, 1):
                c = c + pltpu.roll(c, s, 0)
                m = jnp.maximum(m, pltpu.roll(m, s, 0))
            return c, m

        def snap_cond(c):
            return jnp.max(c[5]) > 0.0

        def snap_body(c):
            cand, chi, t, cge, ngt, pending = c
            ct, nxt = sweep(cand)
            act = pending > 0.0
            ok = ct >= kq
            t = jnp.where(act, cand, t)
            cge = jnp.where(act, ct, cge)
            ngt = jnp.where(act, chi, ngt)
            pending = jnp.where(act & ok, 0.0, pending)
            cand = jnp.where(act & ~ok, nxt, cand)
            chi = jnp.where(act & ~ok, ct, chi)
            return cand, chi, t, cge, ngt, pending

        cand0 = over_tiles(lambda v, kt: jnp.where(v < hix, v, NEG_INF), jnp.max)
        zero = jnp.zeros(st, F32)
        first_sweep = snap_body((cand0, chi, zero, zero, zero, jnp.ones(st, F32)))
        _, _, t_k, cge, ngt, _ = lax.while_loop(snap_cond, snap_body, first_sweep)
        thr[...] = t_k
        jthr[...] = jnp.full(st, float(seq), F32)

        need = kq - ngt
        excess = (cge - ngt) > need

        @pl.when(jnp.max(jnp.where(excess, 1.0, 0.0)) > 0.0)
        def _():
            sub_k = lax.broadcasted_iota(jnp.int32, (tk, nb * tq), 0).astype(F32).reshape(
                tk // SUBLANES, SUBLANES, nb * tq)

            def jbisect(_, c):
                jlo, jhi = c
                mid = jnp.floor(0.5 * (jlo + jhi))
                cnt = over_tiles(
                    lambda v, kt: jnp.where((v == t_k) & (sub_k + float(kt * tk) <= mid), 1.0, 0.0),
                    jnp.sum)
                ge = cnt >= need
                return jnp.where(ge, jlo, mid), jnp.where(ge, mid, jhi)

            n_pass = max(1, int(math.ceil(math.log2(seq))))
            _, jhi = lax.fori_loop(
                0, n_pass, jbisect,
                (jnp.full(st, -1.0, F32), jnp.full(st, float(seq - 1), F32)))
            jthr[...] = jhi

    for n in range(1, seq // tk + 1):
        pl.when(n_kt == n)(functools.partial(search, n))

    for blk in range(nb):
        for i, ref in enumerate((thr, jthr)):
            v = jnp.concatenate([ref[:, blk * tq:(blk + 1) * tq]] * (tq // SUBLANES), axis=0).T
            tj[blk, i] = jnp.concatenate([v] * (tk // tq), axis=1)

    lane_kf = lane_k.astype(F32)

    def attend_tile(kt, carry=None, *, kind, first=False):
        qall = jnp.concatenate(
            [qlat_ref[0, :, blk * tq:(blk + 1) * tq, :].reshape(hq, D_LAT) for blk in range(nb)],
            axis=0)
        k0 = pl.multiple_of(kt * tk, tk)
        ct = caug_ref[0, pl.ds(k0, tk), :]
        src = _dot_nt(qall, ct[:, :D_LAT])
        kf = lane_kf + (kt * tk).astype(F32)
        for blk in range(nb):
            s = sc_std[blk, kt]
            tb = tj[blk, 0]
            sel = (s > tb) | ((s == tb) & (kf <= tj[blk, 1]))
            ma = jnp.where(sel, 0.0, NEG_INF)
            for h in range(H_ATT):
                r0 = (blk * H_ATT + h) * tq
                lh = src[r0:r0 + tq, :] + ma
                if kind == "near" and blk == 0:
                    lh = jnp.concatenate([lh[:, :tq], lh[:, tq:] + btab[h, 1]], axis=1)
                elif kind == "last" and blk == 0:
                    lh = jnp.concatenate([lh[:, :tq] + btab[h, 0], lh[:, tq:]], axis=1)
                elif kind == "last" and blk == 1:
                    lh = lh + jnp.concatenate([btab[h, 1], btab[h, 0]], axis=1)
                mt = jnp.maximum(lh[:, :tq], lh[:, tq:])
                rm = jnp.broadcast_to(jnp.max(mt, axis=1, keepdims=True), (tq, tq))
                if first:
                    m_new = jnp.maximum(rm, M_INIT)
                else:
                    m_old = mrun[blk, h]
                    m_new = jnp.maximum(m_old, rm)
                    alph[blk, h] = jnp.exp2(m_old - m_new)
                mrun[blk, h] = m_new
                m2 = jnp.concatenate([m_new, m_new], axis=1)
                p_s[r0:r0 + tq, :] = jnp.exp2(lh - m2).astype(BF16)
        pv = _dot(p_s[...], ct)
        if first:
            acc[...] = pv
            return carry
        for blk in range(nb):
            for h in range(H_ATT):
                r0 = (blk * H_ATT + h) * tq
                al = alph[blk, h]
                acc[r0:r0 + tq, :] = acc[r0:r0 + tq, :] * jnp.concatenate([al, al], axis=1) + pv[r0:r0 + tq]
        return carry

    attend_tile(step, kind="last", first=True)

    @pl.when(step >= 1)
    def _():
        attend_tile(step - 1, kind="near")

    lax.fori_loop(0, jnp.maximum(step - 1, 0), functools.partial(attend_tile, kind="far"), 0)

    for m in range(H_ATT // 2):
        rows = []
        for blk in range(nb):
            halves = []
            for h in (2 * m, 2 * m + 1):
                r0 = (blk * H_ATT + h) * tq
                a = acc[r0:r0 + tq, :]
                halves.append(a[:, :D_LAT] * (1.0 / a[:, D_LAT:]))
            rows.append(jnp.concatenate(halves, axis=1))
        pair = jnp.concatenate(rows, axis=0).astype(BF16)
        o_ref[0, :, m * 2 * D_V:(m + 1) * 2 * D_V] = _dot(pair, wuv_ref[m]).astype(BF16)


def _attention(rel_bias, qlat, qi, wi, caug, kia, kib, wuv2, *, topk):
    b, _, s, _ = qlat.shape
    tq, tk = Q_BLOCK, K_TILE
    assert s % tk == 0 and tk == 2 * tq
    nb = tk // tq
    n_kt = s // tk
    hq = H_ATT * tq
    di = H_IDX * D_IDX
    dv = H_ATT * D_V
    qrow = lambda i, j: (i, j, 0)
    full = lambda i, j: (i, 0, 0)
    return pl.pallas_call(
        functools.partial(_attn_body, topk=topk, seq=s),
        grid=(b, s // tk),
        in_specs=[
            pl.BlockSpec(memory_space=pltpu.SMEM),
            pl.BlockSpec((1, H_ATT, tk, D_LAT), lambda i, j: (i, 0, j, 0)),
            pl.BlockSpec((1, tk, di), qrow),
            pl.BlockSpec((1, tk, LANES), qrow),
            pl.BlockSpec((1, s, 2 * D_LAT), full),
            pl.BlockSpec((1, s, LANES), full),
            pl.BlockSpec((1, s, LANES), full),
            _const_spec((H_ATT // 2, 2 * D_LAT, 2 * D_V)),
        ],
        out_specs=pl.BlockSpec((1, tk, dv), qrow),
        out_shape=jax.ShapeDtypeStruct((b, s, dv), BF16),
        scratch_shapes=[
            pltpu.VMEM((nb, n_kt, tq, tk), F32),
            pltpu.VMEM((n_kt, tk, nb * tq), F32),
            pltpu.VMEM((nb * hq, tk), F32),
            pltpu.VMEM((nb, H_ATT, tq, tq), F32),
            pltpu.VMEM((nb, H_ATT, tq, tq), F32),
            pltpu.VMEM((H_ATT, 2, tq, tq), F32),
            pltpu.VMEM((nb, H_IDX, tq, tk), F32),
            pltpu.VMEM((nb * hq, tk), BF16),
            pltpu.VMEM((nb, 2, tq, tk), F32),
            pltpu.VMEM((SUBLANES, nb * tq), F32),
            pltpu.VMEM((SUBLANES, nb * tq), F32),
        ],
        compiler_params=pltpu.CompilerParams(
            dimension_semantics=("arbitrary", "arbitrary"), vmem_limit_bytes=VMEM_LIMIT_BYTES),
        name="dsa_attention",
    )(rel_bias, qlat, qi, wi, caug, kia, kib, wuv2)


def _block_diag_pairs(w):
    h, a, b = w.shape
    w = w.reshape(h // 2, 2, a, b)
    z = jnp.zeros((h // 2, a, b), w.dtype)
    top = jnp.concatenate([w[:, 0], z], axis=2)
    bot = jnp.concatenate([z, w[:, 1]], axis=2)
    return jnp.concatenate([top, bot], axis=1)


def _pack_att_w(w_in):
    dq = H_ATT * D_QK
    di = H_IDX * D_IDX
    d = w_in.shape[0]
    o = dq + D_LAT + di
    w_ki = w_in[:, o:o + D_IDX]
    w_wi = w_in[:, o + D_IDX:o + D_IDX + H_IDX]
    z_ki = jnp.zeros((d, LANES - D_IDX), w_in.dtype)
    z_wi = jnp.zeros((d, LANES - H_IDX), w_in.dtype)
    return jnp.concatenate([w_in[:, :o], w_ki, z_ki, z_ki, w_ki, w_wi, z_wi], axis=1)


def kernel(x, norm_mix_g, norm_mlp_g, final_norm_g, rec_w_in, rec_conv_w, rec_conv_b, rec_w_a, rec_b_a, rec_w_x, rec_b_x, rec_lambda, rec_w_out, att_w_in, att_kv_norm_g, att_w_uk, att_w_uv, att_w_o, rel_bias, mlp_w_up, mlp_w_down):
    b, s, d = x.shape
    depth = norm_mix_g.shape[0]
    topk = min(TOPK_MAX, s // 4)
    row = lambda v: v.reshape(1, -1).astype(F32)
    for layer in range(depth):
        j = layer // 2
        g_mix = row(norm_mix_g[layer])
        final = layer == depth - 1
        mlp_w = (row(norm_mlp_g[layer]), mlp_w_up[layer].astype(BF16), mlp_w_down[layer].astype(BF16),
                 row(final_norm_g))
        if layer % 2 == 0:
            x = _rec_mlp_layer(
                x, g_mix, rec_w_in[j].astype(BF16), rec_conv_w[j].astype(F32), row(rec_conv_b[j]),
                rec_w_a[j].astype(BF16), row(rec_b_a[j]), rec_w_x[j].astype(BF16), row(rec_b_x[j]),
                row(rec_lambda[j]), rec_w_out[j].astype(BF16), *mlp_w, final=final)
        else:
            qlat, caug, qi, kia, kib, wi = _att_proj(
                x, g_mix, _pack_att_w(att_w_in[j]).astype(BF16), row(att_kv_norm_g[j]),
                _block_diag_pairs(att_w_uk[j]).astype(BF16))
            y = _attention(
                rel_bias.astype(F32), qlat, qi, wi, caug, kia, kib,
                _block_diag_pairs(att_w_uv[j]).astype(BF16), topk=topk)
            x = _mlp_layer(
                x.reshape(b * s, d), y.reshape(b * s, d), att_w_o[j].astype(BF16), *mlp_w,
                final=final).reshape(b, s, d)
    return x
```

```python
import functools
import math

import jax
import jax.numpy as jnp
from jax import lax
from jax.experimental import pallas as pl
from jax.experimental.pallas import tpu as pltpu

F32 = jnp.float32
BF16 = jnp.bfloat16

EPS = 1e-6
RG_BLOCKS = 4
CONV_W = 4
RG_C = 8.0
H_ATT = 16
D_QK = 64
D_V = 64
D_LAT = 128
H_IDX = 8
D_IDX = 64
TOPK_MAX = 256
ATT_SCALE = D_QK ** -0.5
LOG2E = math.log2(math.e)
IDX_W_SCALE = (H_IDX ** -0.5) * (D_IDX ** -0.5)
NUM_BUCKETS = 32
MAX_DISTANCE = 128
MAX_EXACT = NUM_BUCKETS // 2

LANES = 128
SUBLANES = 8
VMEM_LIMIT_BYTES = 56 * 1024 * 1024

Q_BLOCK = 128
K_TILE = 256
N_BISECT = 14
NEG_INF = float("-inf")
POS_INF = float("inf")
M_INIT = -1e30


def _rms(x, g):
    ms = jnp.mean(x * x, axis=-1, keepdims=True)
    return x * lax.rsqrt(ms + EPS) * g


def _dot(a, b):
    return jnp.dot(a, b, preferred_element_type=F32)


def _dot_nt(a, b):
    return lax.dot_general(a, b, (((1,), (1,)), ((), ())), preferred_element_type=F32)


def _const_spec(shape):
    nd = len(shape)
    return pl.BlockSpec(shape, lambda *_: (0,) * nd, pipeline_mode=pl.Buffered(1))


def _mlp_tile(x, y, wpre_ref, g_ref, wup_ref, wdn_ref, gfin_ref, *, ff_chunk, final):
    x1 = x + _dot(y, wpre_ref[...])
    hn = _rms(x1, g_ref[...]).astype(BF16)
    acc = x1
    d_ff = wup_ref.shape[1]
    for c in range(d_ff // ff_chunk):
        h = _dot(hn, wup_ref[:, c * ff_chunk:(c + 1) * ff_chunk])
        h = jnp.maximum(h, 0.0)
        h = (h * h).astype(BF16)
        acc = acc + _dot(h, wdn_ref[c * ff_chunk:(c + 1) * ff_chunk, :])
    if final:
        acc = _rms(acc, gfin_ref[...])
    return acc


def _mlp_body(x_ref, y_ref, wpre_ref, g_ref, wup_ref, wdn_ref, gfin_ref, o_ref, *, ff_chunk, final):
    o_ref[...] = _mlp_tile(x_ref[...], y_ref[...], wpre_ref, g_ref, wup_ref, wdn_ref, gfin_ref,
                           ff_chunk=ff_chunk, final=final)


def _mlp_layer(x2d, y2d, w_pre, g, w_up, w_dn, g_fin, *, final, tm=512, ff_chunk=512):
    n, d = x2d.shape
    d_ff = w_up.shape[1]
    tm = min(tm, n)
    return pl.pallas_call(
        functools.partial(_mlp_body, ff_chunk=min(ff_chunk, d_ff), final=final),
        grid=(n // tm,),
        in_specs=[
            pl.BlockSpec((tm, d), lambda i: (i, 0)),
            pl.BlockSpec((tm, d), lambda i: (i, 0)),
            _const_spec((d, d)),
            _const_spec((1, d)),
            _const_spec((d, d_ff)),
            _const_spec((d_ff, d)),
            _const_spec((1, d)),
        ],
        out_specs=pl.BlockSpec((tm, d), lambda i: (i, 0)),
        out_shape=jax.ShapeDtypeStruct((n, d), F32),
        compiler_params=pltpu.CompilerParams(
            dimension_semantics=("arbitrary",), vmem_limit_bytes=VMEM_LIMIT_BYTES),
        name="mlp_block",
    )(x2d, y2d, w_pre, g, w_up, w_dn, g_fin)


def _gelu_tanh(x):
    c = math.sqrt(2.0 / math.pi)
    return 0.5 * x * (1.0 + jnp.tanh(c * (x + 0.044715 * (x * x * x))))


def _sigmoid(x):
    return 1.0 / (1.0 + jnp.exp(-x))


def _rec_block(n, yg, xr, cw_ref, cb_ref, wa_ref, ba_ref, wx_ref, bx_ref, lam_ref,
               xbuf, hcar, y_out):
    ts, bw = yg.shape
    cs = slice(n * bw, (n + 1) * bw)
    gate = _gelu_tanh(yg)

    xbuf[SUBLANES:SUBLANES + ts, cs] = xr
    cw = cw_ref[:, cs]
    xc = cb_ref[:, cs] + xbuf[pl.ds(SUBLANES - 3, ts), cs] * cw[0:1]
    xc = xc + xbuf[pl.ds(SUBLANES - 2, ts), cs] * cw[1:2]
    xc = xc + xbuf[pl.ds(SUBLANES - 1, ts), cs] * cw[2:3]
    xc = xc + xr * cw[3:4]
    xbuf[0:SUBLANES, cs] = xbuf[ts:ts + SUBLANES, cs]

    xcb = xc.astype(BF16)
    r = _sigmoid(_dot(xcb, wa_ref[n]) + ba_ref[:, cs])
    ig = _sigmoid(_dot(xcb, wx_ref[n]) + bx_ref[:, cs])
    nl = -lam_ref[:, cs]
    softplus = jnp.maximum(nl, 0.0) + jnp.log1p(jnp.exp(-jnp.abs(nl)))
    log_a = (-RG_C * r) * softplus
    a = jnp.exp(log_a)
    v = -jnp.tanh(log_a) * (a * a + 1.0)
    root = jnp.where(v > 0.0, v * lax.rsqrt(v), 0.0)
    u = root * (ig * xc)

    row = lax.broadcasted_iota(jnp.int32, (SUBLANES, bw), 0)
    masks = [(s, row >= s) for s in (1, 2, 4)]
    hp = hcar[:, cs]
    hs = []
    for c in range(ts // SUBLANES):
        av = a[c * SUBLANES:(c + 1) * SUBLANES]
        uv = u[c * SUBLANES:(c + 1) * SUBLANES]
        for s, m in masks:
            a_sh = jnp.where(m, pltpu.roll(av, s, 0), 1.0)
            u_sh = jnp.where(m, pltpu.roll(uv, s, 0), 0.0)
            uv = av * u_sh + uv
            av = av * a_sh
        hv = av * hp + uv
        hs.append(hv)
        hp = hv[SUBLANES - 1:SUBLANES, :]
    hcar[:, cs] = hp
    y_out[:, cs] = (jnp.concatenate(hs, axis=0) * gate).astype(BF16)


def _rec_mlp_body(x_ref, xp_ref, g_ref, win_ref, cw_ref, cb_ref, wa_ref, ba_ref, wx_ref, bx_ref,
                  lam_ref, wpre_ref, gm_ref, wup_ref, wdn_ref, gfin_ref, o_ref,
                  xbuf, hcar, y_s, *, ts, tiles_per_seq, ff_chunk, final):
    t = pl.program_id(0)
    d = x_ref.shape[-1]
    d_ff = wup_ref.shape[1]
    n_chunks = d_ff // ff_chunk

    @pl.when(lax.rem(t, tiles_per_seq) == 0)
    def _():
        xbuf[0:SUBLANES, :] = jnp.zeros((SUBLANES, d), F32)
        hcar[...] = jnp.zeros((1, d), F32)

    @pl.when(t == 0)
    def _():
        y_s[...] = jnp.zeros(y_s.shape, BF16)

    y2 = _dot(_rms(x_ref[...], g_ref[...]).astype(BF16), win_ref[...])
    bw = d // RG_BLOCKS
    x1 = xp_ref[...] + _dot(y_s[...], wpre_ref[...])
    hn = _rms(x1, gm_ref[...]).astype(BF16)
    acc = x1
    hs = {}

    def up(c):
        h = jnp.maximum(_dot(hn, wup_ref[:, c * ff_chunk:(c + 1) * ff_chunk]), 0.0)
        hs[c] = (h * h).astype(BF16)

    def down(c):
        return _dot(hs.pop(c), wdn_ref[c * ff_chunk:(c + 1) * ff_chunk, :])

    mlp_ops = []
    for c in range(n_chunks):
        if c + 1 < n_chunks:
            mlp_ops.append((up, c + 1))
        mlp_ops.append((down, c))
    up(0)
    for n in range(RG_BLOCKS):
        _rec_block(n, y2[:, n * bw:(n + 1) * bw], y2[:, d + n * bw:d + (n + 1) * bw],
                   cw_ref, cb_ref, wa_ref, ba_ref, wx_ref, bx_ref, lam_ref, xbuf, hcar, y_s)
        take = -(-len(mlp_ops) // (RG_BLOCKS - n))
        for fn, c in mlp_ops[:take]:
            r = fn(c)
            if r is not None:
                acc = acc + r
        mlp_ops = mlp_ops[take:]
    if final:
        acc = _rms(acc, gfin_ref[...])
    o_ref[...] = acc


def _rec_mlp_layer(x, g, w_in, conv_w, conv_b, w_a, b_a, w_x, b_x, lam,
                   w_pre, g_mlp, w_up, w_dn, g_fin, *, final, ts=256, ff_chunk=1024):
    b, s, d = x.shape
    ts = min(ts, s)
    bw = d // RG_BLOCKS
    d_ff = w_up.shape[1]
    n_tiles = b * s // ts
    x2d = x.reshape(b * s, d)
    cur = lambda t: (jnp.minimum(t, n_tiles - 1), 0)
    prev = lambda t: (jnp.maximum(t - 1, 0), 0)
    out = pl.pallas_call(
        functools.partial(_rec_mlp_body, ts=ts, tiles_per_seq=s // ts,
                          ff_chunk=min(ff_chunk, d_ff), final=final),
        grid=(n_tiles + 1,),
        in_specs=[
            pl.BlockSpec((ts, d), cur),
            pl.BlockSpec((ts, d), prev),
            _const_spec((1, d)),
            _const_spec((d, 2 * d)),
            _const_spec((CONV_W, d)),
            _const_spec((1, d)),
            _const_spec((RG_BLOCKS, bw, bw)),
            _const_spec((1, d)),
            _const_spec((RG_BLOCKS, bw, bw)),
            _const_spec((1, d)),
            _const_spec((1, d)),
            _const_spec((d, d)),
            _const_spec((1, d)),
            _const_spec((d, d_ff)),
            _const_spec((d_ff, d)),
            _const_spec((1, d)),
        ],
        out_specs=pl.BlockSpec((ts, d), prev),
        out_shape=jax.ShapeDtypeStruct((b * s, d), F32),
        scratch_shapes=[
            pltpu.VMEM((ts + SUBLANES, d), F32),
            pltpu.VMEM((1, d), F32),
            pltpu.VMEM((ts, d), BF16),
        ],
        compiler_params=pltpu.CompilerParams(
            dimension_semantics=("arbitrary",), vmem_limit_bytes=VMEM_LIMIT_BYTES,
        ),
        name="rglru_mlp",
    )(x2d, x2d, g, w_in, conv_w, conv_b, w_a, b_a, w_x, b_x, lam, w_pre, g_mlp, w_up, w_dn, g_fin)
    return out.reshape(b, s, d)


def _attproj_body(x_ref, g_ref, w_ref, gkv_ref, wuk_ref,
                  qlat_ref, caug_ref, qi_ref, kia_ref, kib_ref, wi_ref):
    dq = H_ATT * D_QK
    di = H_IDX * D_IDX
    hn = _rms(x_ref[0], g_ref[...]).astype(BF16)
    y = _dot(hn, w_ref[...])
    ts = y.shape[0]
    o = dq
    craw = y[:, o:o + D_LAT]
    o += D_LAT
    qi_ref[0] = y[:, o:o + di].astype(BF16)
    o += di
    tail = y[:, o:o + LANES]
    moved = pltpu.roll(tail, D_IDX, 1)
    lane = lax.broadcasted_iota(jnp.int32, tail.shape, 1)
    kia_ref[0] = jnp.where(lane < D_IDX, tail, 0.0).astype(BF16)
    kib_ref[0] = jnp.where(lane >= D_IDX, moved, 0.0).astype(BF16)
    wi_ref[0] = jnp.where(lane < H_IDX, moved, 0.0) * IDX_W_SCALE

    c = _rms(craw, gkv_ref[...])
    caug_ref[0] = jnp.concatenate([c, jnp.ones((ts, D_LAT), F32)], axis=1).astype(BF16)

    for m in range(H_ATT // 2):
        qp = y[:, m * LANES:(m + 1) * LANES].astype(BF16)
        ql = _dot(qp, wuk_ref[m]) * (ATT_SCALE * LOG2E)
        qlat_ref[0, 2 * m] = ql[:, :D_LAT].astype(BF16)
        qlat_ref[0, 2 * m + 1] = ql[:, D_LAT:].astype(BF16)


def _att_proj(x, g, w_pack, g_kv, wuk2, *, ts=256):
    b, s, d = x.shape
    ts = min(ts, s)
    n_out = w_pack.shape[1]
    di = H_IDX * D_IDX
    row = lambda i, j: (i, j, 0)
    return pl.pallas_call(
        _attproj_body,
        grid=(b, s // ts),
        in_specs=[
            pl.BlockSpec((1, ts, d), row),
            _const_spec((1, d)),
            _const_spec((d, n_out)),
            _const_spec((1, D_LAT)),
            _const_spec((H_ATT // 2, 2 * D_QK, 2 * D_LAT)),
        ],
        out_specs=[
            pl.BlockSpec((1, H_ATT, ts, D_LAT), lambda i, j: (i, 0, j, 0)),
            pl.BlockSpec((1, ts, 2 * D_LAT), row),
            pl.BlockSpec((1, ts, di), row),
            pl.BlockSpec((1, ts, LANES), row),
            pl.BlockSpec((1, ts, LANES), row),
            pl.BlockSpec((1, ts, LANES), row),
        ],
        out_shape=[
            jax.ShapeDtypeStruct((b, H_ATT, s, D_LAT), BF16),
            jax.ShapeDtypeStruct((b, s, 2 * D_LAT), BF16),
            jax.ShapeDtypeStruct((b, s, di), BF16),
            jax.ShapeDtypeStruct((b, s, LANES), BF16),
            jax.ShapeDtypeStruct((b, s, LANES), BF16),
            jax.ShapeDtypeStruct((b, s, LANES), F32),
        ],
        compiler_params=pltpu.CompilerParams(
            dimension_semantics=("arbitrary", "arbitrary"), vmem_limit_bytes=VMEM_LIMIT_BYTES),
        name="att_proj",
    )(x, g, w_pack, g_kv, wuk2)


def _t5_bucket(dist):
    n = jnp.maximum(dist, 0)
    nf = jnp.maximum(n, 1).astype(F32)
    large = MAX_EXACT + jnp.floor(jnp.log(nf / MAX_EXACT) / math.log(MAX_DISTANCE / MAX_EXACT)
                                  * (NUM_BUCKETS - MAX_EXACT)).astype(jnp.int32)
    large = jnp.minimum(large, NUM_BUCKETS - 1)
    return jnp.where(n < MAX_EXACT, n, large)


def _attn_body(rb_ref, qlat_ref, qi_ref, wi_ref, caug_ref, kia_ref, kib_ref, wuv_ref, o_ref,
               sc_std, sc_t, acc, mrun, alph, btab, wib, p_s, tj, thr, jthr, *, topk, seq):
    tq, tk = Q_BLOCK, K_TILE
    nb = tk // tq
    step = pl.program_id(1)
    n_kt = step + 1
    hq = H_ATT * tq
    pairs = H_IDX // 2

    @pl.when((pl.program_id(0) == 0) & (step == 0))
    def _():
        ql = lax.broadcasted_iota(jnp.int32, (tq, tq), 0)
        kl = lax.broadcasted_iota(jnp.int32, (tq, tq), 1)
        for var in range(2):
            bucket = _t5_bucket(ql - kl + var * tq)
            for h in range(H_ATT):
                t = jnp.zeros((tq, tq), F32)
                for bk in range(NUM_BUCKETS - 1):
                    t = jnp.where(bucket == bk, rb_ref[bk, h] - rb_ref[NUM_BUCKETS - 1, h], t)
                btab[h, var] = t * LOG2E

    qi = qi_ref[0]
    qi8 = jnp.concatenate(
        [qi[blk * tq:(blk + 1) * tq, m * LANES:(m + 1) * LANES]
         for blk in range(nb) for m in range(pairs)], axis=0)
    wi = wi_ref[0]
    for blk in range(nb):
        for h in range(H_IDX):
            wib[blk, h] = jnp.broadcast_to(wi[blk * tq:(blk + 1) * tq, h:h + 1], (tq, tk))
    row_q = lax.broadcasted_iota(jnp.int32, (tq, tk), 0)
    lane_k = lax.broadcasted_iota(jnp.int32, (tq, tk), 1)

    def score_tile(kt, carry, *, last):
        k0 = pl.multiple_of(kt * tk, tk)
        da = jnp.maximum(_dot_nt(qi8, kia_ref[0, pl.ds(k0, tk), :]), 0.0)
        db = jnp.maximum(_dot_nt(qi8, kib_ref[0, pl.ds(k0, tk), :]), 0.0)
        for blk in range(nb):
            s = jnp.zeros((tq, tk), F32)
            for m in range(pairs):
                r0 = (blk * pairs + m) * tq
                s = s + wib[blk, 2 * m] * da[r0:r0 + tq]
                s = s + wib[blk, 2 * m + 1] * db[r0:r0 + tq]
            if last:
                s = jnp.where(lane_k <= row_q + blk * tq, s, NEG_INF)
            sc_std[blk, kt] = s
            sc_t[kt, :, blk * tq:(blk + 1) * tq] = s.T
        return carry

    def score_pair(j, carry):
        score_tile(2 * j, carry, last=False)
        return score_tile(2 * j + 1, carry, last=False)

    lax.fori_loop(0, lax.shift_right_logical(step, 1), score_pair, 0)

    @pl.when((step & 1) == 1)
    def _():
        score_tile(step - 1, 0, last=False)

    score_tile(step, 0, last=True)

    st = (SUBLANES, nb * tq)
    p_lane = step * (nb * tq) + lax.broadcasted_iota(jnp.int32, st, 1)
    kq = jnp.minimum(topk, p_lane + 1).astype(F32)

    def search(n):
        def over_tiles(fn, op):
            pair = {jnp.sum: jnp.add, jnp.max: jnp.maximum, jnp.min: jnp.minimum}[op]
            parts = [op(fn(sc_t[kt].reshape(tk // SUBLANES, SUBLANES, nb * tq), kt), axis=0)
                     for kt in range(n)]
            while len(parts) > 1:
                parts = [pair(*parts[i:i + 2]) if i + 1 < len(parts) else parts[i]
                         for i in range(0, len(parts), 2)]
            x = parts[0]
            for s in (4, 2, 1):
                x = pair(x, pltpu.roll(x, s, 0))
            return x

        def count_ge(th):
            return over_tiles(lambda v, kt: jnp.where(v >= th, 1.0, 0.0), jnp.sum)

        mx = over_tiles(lambda v, kt: v, jnp.max)
        lo0 = over_tiles(lambda v, kt: jnp.where(v == NEG_INF, POS_INF, v), jnp.min)

        def bisect(_, c):
            lo, hix, chi = c
            hib = jnp.where(hix == POS_INF, mx, hix)
            mid = 0.5 * lo + 0.5 * hib
            cnt = count_ge(mid)
            ge = cnt >= kq
            return jnp.where(ge, mid, lo), jnp.where(ge, hix, mid), jnp.where(ge, chi, cnt)

        _, hix, chi = lax.fori_loop(
            0, N_BISECT, bisect,
            (lo0, jnp.full(st, POS_INF, F32), jnp.zeros(st, F32)))

        def sweep(cand):
            cs, ms = [], []
            for kt in range(n):
                v = sc_t[kt].reshape(tk // SUBLANES, SUBLANES, nb * tq)
                ge = v >= cand
                cs.append(jnp.sum(jnp.where(ge, 1.0, 0.0), axis=0))
                ms.append(jnp.max(jnp.where(ge, NEG_INF, v), axis=0))
            c, m = functools.reduce(jnp.add, cs), functools.reduce(jnp.maximum, ms)
            for s in (4, 2, 1):
                c = c + pltpu.roll(c, s, 0)
                m = jnp.maximum(m, pltpu.roll(m, s, 0))
            return c, m

        def snap_cond(c):
            return jnp.max(c[5]) > 0.0

        def snap_body(c):
            cand, chi, t, cge, ngt, pending = c
            ct, nxt = sweep(cand)
            act = pending > 0.0
            ok = ct >= kq
            t = jnp.where(act, cand, t)
            cge = jnp.where(act, ct, cge)
            ngt = jnp.where(act, chi, ngt)
            pending = jnp.where(act & ok, 0.0, pending)
            cand = jnp.where(act & ~ok, nxt, cand)
            chi = jnp.where(act & ~ok, ct, chi)
            return cand, chi, t, cge, ngt, pending

        cand0 = over_tiles(lambda v, kt: jnp.where(v < hix, v, NEG_INF), jnp.max)
        zero = jnp.zeros(st, F32)
        first_sweep = snap_body((cand0, chi, zero, zero, zero, jnp.ones(st, F32)))
        _, _, t_k, cge, ngt, _ = lax.while_loop(snap_cond, snap_body, first_sweep)
        thr[...] = t_k
        jthr[...] = jnp.full(st, float(seq), F32)

        need = kq - ngt
        excess = (cge - ngt) > need

        @pl.when(jnp.max(jnp.where(excess, 1.0, 0.0)) > 0.0)
        def _():
            sub_k = lax.broadcasted_iota(jnp.int32, (tk, nb * tq), 0).astype(F32).reshape(
                tk // SUBLANES, SUBLANES, nb * tq)

            def jbisect(_, c):
                jlo, jhi = c
                mid = jnp.floor(0.5 * (jlo + jhi))
                cnt = over_tiles(
                    lambda v, kt: jnp.where((v == t_k) & (sub_k + float(kt * tk) <= mid), 1.0, 0.0),
                    jnp.sum)
                ge = cnt >= need
                return jnp.where(ge, jlo, mid), jnp.where(ge, mid, jhi)

            n_pass = max(1, int(math.ceil(math.log2(seq))))
            _, jhi = lax.fori_loop(
                0, n_pass, jbisect,
                (jnp.full(st, -1.0, F32), jnp.full(st, float(seq - 1), F32)))
            jthr[...] = jhi

    for n in range(1, seq // tk + 1):
        pl.when(n_kt == n)(functools.partial(search, n))

    for blk in range(nb):
        for i, ref in enumerate((thr, jthr)):
            v = jnp.concatenate([ref[:, blk * tq:(blk + 1) * tq]] * (tq // SUBLANES), axis=0).T
            tj[blk, i] = jnp.concatenate([v] * (tk // tq), axis=1)

    lane_kf = lane_k.astype(F32)

    def attend_tile(kt, carry=None, *, kind, first=False):
        qall = jnp.concatenate(
            [qlat_ref[0, :, blk * tq:(blk + 1) * tq, :].reshape(hq, D_LAT) for blk in range(nb)],
            axis=0)
        k0 = pl.multiple_of(kt * tk, tk)
        ct = caug_ref[0, pl.ds(k0, tk), :]
        src = _dot_nt(qall, ct[:, :D_LAT])
        kf = lane_kf + (kt * tk).astype(F32)
        for blk in range(nb):
            s = sc_std[blk, kt]
            tb = tj[blk, 0]
            sel = (s > tb) | ((s == tb) & (kf <= tj[blk, 1]))
            ma = jnp.where(sel, 0.0, NEG_INF)
            for h in range(H_ATT):
                r0 = (blk * H_ATT + h) * tq
                lh = src[r0:r0 + tq, :] + ma
                if kind == "near" and blk == 0:
                    lh = jnp.concatenate([lh[:, :tq], lh[:, tq:] + btab[h, 1]], axis=1)
                elif kind == "last" and blk == 0:
                    lh = jnp.concatenate([lh[:, :tq] + btab[h, 0], lh[:, tq:]], axis=1)
                elif kind == "last" and blk == 1:
                    lh = lh + jnp.concatenate([btab[h, 1], btab[h, 0]], axis=1)
                mt = jnp.maximum(lh[:, :tq], lh[:, tq:])
                rm = jnp.broadcast_to(jnp.max(mt, axis=1, keepdims=True), (tq, tq))
                if first:
                    m_new = jnp.maximum(rm, M_INIT)
                else:
                    m_old = mrun[blk, h]
                    m_new = jnp.maximum(m_old, rm)
                    alph[blk, h] = jnp.exp2(m_old - m_new)
                mrun[blk, h] = m_new
                m2 = jnp.concatenate([m_new, m_new], axis=1)
                p_s[r0:r0 + tq, :] = jnp.exp2(lh - m2).astype(BF16)
        pv = _dot(p_s[...], ct)
        if first:
            acc[...] = pv
            return carry
        for blk in range(nb):
            for h in range(H_ATT):
                r0 = (blk * H_ATT + h) * tq
                al = alph[blk, h]
                acc[r0:r0 + tq, :] = acc[r0:r0 + tq, :] * jnp.concatenate([al, al], axis=1) + pv[r0:r0 + tq]
        return carry

    attend_tile(step, kind="last", first=True)

    @pl.when(step >= 1)
    def _():
        attend_tile(step - 1, kind="near")

    lax.fori_loop(0, jnp.maximum(step - 1, 0), functools.partial(attend_tile, kind="far"), 0)

    for m in range(H_ATT // 2):
        rows = []
        for blk in range(nb):
            halves = []
            for h in (2 * m, 2 * m + 1):
                r0 = (blk * H_ATT + h) * tq
                a = acc[r0:r0 + tq, :]
                halves.append(a[:, :D_LAT] * (1.0 / a[:, D_LAT:]))
            rows.append(jnp.concatenate(halves, axis=1))
        pair = jnp.concatenate(rows, axis=0).astype(BF16)
        o_ref[0, :, m * 2 * D_V:(m + 1) * 2 * D_V] = _dot(pair, wuv_ref[m]).astype(BF16)


def _attention(rel_bias, qlat, qi, wi, caug, kia, kib, wuv2, *, topk):
    b, _, s, _ = qlat.shape
    tq, tk = Q_BLOCK, K_TILE
    assert s % tk == 0 and tk == 2 * tq
    nb = tk // tq
    n_kt = s // tk
    hq = H_ATT * tq
    di = H_IDX * D_IDX
    dv = H_ATT * D_V
    qrow = lambda i, j: (i, j, 0)
    full = lambda i, j: (i, 0, 0)
    return pl.pallas_call(
        functools.partial(_attn_body, topk=topk, seq=s),
        grid=(b, s // tk),
        in_specs=[
            pl.BlockSpec(memory_space=pltpu.SMEM),
            pl.BlockSpec((1, H_ATT, tk, D_LAT), lambda i, j: (i, 0, j, 0)),
            pl.BlockSpec((1, tk, di), qrow),
            pl.BlockSpec((1, tk, LANES), qrow),
            pl.BlockSpec((1, s, 2 * D_LAT), full),
            pl.BlockSpec((1, s, LANES), full),
            pl.BlockSpec((1, s, LANES), full),
            _const_spec((H_ATT // 2, 2 * D_LAT, 2 * D_V)),
        ],
        out_specs=pl.BlockSpec((1, tk, dv), qrow),
        out_shape=jax.ShapeDtypeStruct((b, s, dv), BF16),
        scratch_shapes=[
            pltpu.VMEM((nb, n_kt, tq, tk), F32),
            pltpu.VMEM((n_kt, tk, nb * tq), F32),
            pltpu.VMEM((nb * hq, tk), F32),
            pltpu.VMEM((nb, H_ATT, tq, tq), F32),
            pltpu.VMEM((nb, H_ATT, tq, tq), F32),
            pltpu.VMEM((H_ATT, 2, tq, tq), F32),
            pltpu.VMEM((nb, H_IDX, tq, tk), F32),
            pltpu.VMEM((nb * hq, tk), BF16),
            pltpu.VMEM((nb, 2, tq, tk), F32),
            pltpu.VMEM((SUBLANES, nb * tq), F32),
            pltpu.VMEM((SUBLANES, nb * tq), F32),
        ],
        compiler_params=pltpu.CompilerParams(
            dimension_semantics=("arbitrary", "arbitrary"), vmem_limit_bytes=VMEM_LIMIT_BYTES),
        name="dsa_attention",
    )(rel_bias, qlat, qi, wi, caug, kia, kib, wuv2)


def _block_diag_pairs(w):
    h, a, b = w.shape
    w = w.reshape(h // 2, 2, a, b)
    z = jnp.zeros((h // 2, a, b), w.dtype)
    top = jnp.concatenate([w[:, 0], z], axis=2)
    bot = jnp.concatenate([z, w[:, 1]], axis=2)
    return jnp.concatenate([top, bot], axis=1)


def _pack_att_w(w_in):
    assert 2 * D_IDX == LANES and D_IDX + H_IDX <= LANES
    n = w_in.shape[1]
    return jnp.pad(w_in, ((0, 0), (0, -n % LANES)))


def kernel(x, norm_mix_g, norm_mlp_g, final_norm_g, rec_w_in, rec_conv_w, rec_conv_b, rec_w_a, rec_b_a, rec_w_x, rec_b_x, rec_lambda, rec_w_out, att_w_in, att_kv_norm_g, att_w_uk, att_w_uv, att_w_o, rel_bias, mlp_w_up, mlp_w_down):
    b, s, d = x.shape
    depth = norm_mix_g.shape[0]
    topk = min(TOPK_MAX, s // 4)
    row = lambda v: v.reshape(1, -1).astype(F32)
    for layer in range(depth):
        j = layer // 2
        g_mix = row(norm_mix_g[layer])
        final = layer == depth - 1
        mlp_w = (row(norm_mlp_g[layer]), mlp_w_up[layer].astype(BF16), mlp_w_down[layer].astype(BF16),
                 row(final_norm_g))
        if layer % 2 == 0:
            x = _rec_mlp_layer(
                x, g_mix, rec_w_in[j].astype(BF16), rec_conv_w[j].astype(F32), row(rec_conv_b[j]),
                rec_w_a[j].astype(BF16), row(rec_b_a[j]), rec_w_x[j].astype(BF16), row(rec_b_x[j]),
                row(rec_lambda[j]), rec_w_out[j].astype(BF16), *mlp_w, final=final)
        else:
            qlat, caug, qi, kia, kib, wi = _att_proj(
                x, g_mix, _pack_att_w(att_w_in[j].astype(BF16)), row(att_kv_norm_g[j]),
                _block_diag_pairs(att_w_uk[j]).astype(BF16))
            y = _attention(
                rel_bias.astype(F32), qlat, qi, wi, caug, kia, kib,
                _block_diag_pairs(att_w_uv[j]).astype(BF16), topk=topk)
            x = _mlp_layer(
                x.reshape(b * s, d), y.reshape(b * s, d), att_w_o[j].astype(BF16), *mlp_w,
                final=final).reshape(b, s, d)
    return x
```

```python
import functools
import math

import jax
import jax.numpy as jnp
from jax import lax
from jax.experimental import pallas as pl
from jax.experimental.pallas import tpu as pltpu

F32 = jnp.float32
BF16 = jnp.bfloat16

EPS = 1e-6
RG_BLOCKS = 4
CONV_W = 4
RG_C = 8.0
H_ATT = 16
D_QK = 64
D_V = 64
D_LAT = 128
H_IDX = 8
D_IDX = 64
TOPK_MAX = 256
ATT_SCALE = D_QK ** -0.5
LOG2E = math.log2(math.e)
IDX_W_SCALE = (H_IDX ** -0.5) * (D_IDX ** -0.5)
NUM_BUCKETS = 32
MAX_DISTANCE = 128
MAX_EXACT = NUM_BUCKETS // 2

LANES = 128
SUBLANES = 8
VMEM_LIMIT_BYTES = 56 * 1024 * 1024

Q_BLOCK = 128
K_TILE = 256
N_BISECT = 14
NEG_INF = float("-inf")
POS_INF = float("inf")
M_INIT = -1e30


def _rms(x, g):
    ms = jnp.mean(x * x, axis=-1, keepdims=True)
    return x * lax.rsqrt(ms + EPS) * g


def _dot(a, b):
    return jnp.dot(a, b, preferred_element_type=F32)


def _dot_nt(a, b):
    return lax.dot_general(a, b, (((1,), (1,)), ((), ())), preferred_element_type=F32)


def _const_spec(shape):
    nd = len(shape)
    return pl.BlockSpec(shape, lambda *_: (0,) * nd, pipeline_mode=pl.Buffered(1))


def _layer_spec(shape, layer):
    nd = len(shape)
    return pl.BlockSpec((None, *shape), lambda *_: (layer,) + (0,) * nd,
                        pipeline_mode=pl.Buffered(1))


def _mlp_tile(x, y, wpre_ref, g_ref, wup_ref, wdn_ref, gfin_ref, *, ff_chunk, final):
    x1 = x + _dot(y, wpre_ref[...])
    hn = _rms(x1, g_ref[...]).astype(BF16)
    acc = x1
    d_ff = wup_ref.shape[1]
    for c in range(d_ff // ff_chunk):
        h = _dot(hn, wup_ref[:, c * ff_chunk:(c + 1) * ff_chunk])
        h = jnp.maximum(h, 0.0)
        h = (h * h).astype(BF16)
        acc = acc + _dot(h, wdn_ref[c * ff_chunk:(c + 1) * ff_chunk, :])
    if final:
        acc = _rms(acc, gfin_ref[...])
    return acc


def _mlp_body(x_ref, y_ref, wpre_ref, g_ref, wup_ref, wdn_ref, gfin_ref, o_ref, *, ff_chunk, final):
    o_ref[...] = _mlp_tile(x_ref[...], y_ref[...], wpre_ref, g_ref, wup_ref, wdn_ref, gfin_ref,
                           ff_chunk=ff_chunk, final=final)


def _mlp_layer(x2d, y2d, w_pre, g, w_up, w_dn, g_fin, *, layer, final, tm=512, ff_chunk=512):
    n, d = x2d.shape
    d_ff = w_up.shape[-1]
    tm = min(tm, n)
    return pl.pallas_call(
        functools.partial(_mlp_body, ff_chunk=min(ff_chunk, d_ff), final=final),
        grid=(n // tm,),
        in_specs=[
            pl.BlockSpec((tm, d), lambda i: (i, 0)),
            pl.BlockSpec((tm, d), lambda i: (i, 0)),
            _const_spec((d, d)),
            _const_spec((1, d)),
            _layer_spec((d, d_ff), layer),
            _layer_spec((d_ff, d), layer),
            _const_spec((1, d)),
        ],
        out_specs=pl.BlockSpec((tm, d), lambda i: (i, 0)),
        out_shape=jax.ShapeDtypeStruct((n, d), F32),
        compiler_params=pltpu.CompilerParams(
            dimension_semantics=("arbitrary",), vmem_limit_bytes=VMEM_LIMIT_BYTES),
        name="mlp_block",
    )(x2d, y2d, w_pre, g, w_up, w_dn, g_fin)


def _gelu_tanh(x):
    c = math.sqrt(2.0 / math.pi)
    return 0.5 * x * (1.0 + jnp.tanh(c * (x + 0.044715 * (x * x * x))))


def _sigmoid(x):
    return 1.0 / (1.0 + jnp.exp(-x))


def _rec_block(n, yg, xr, cw_ref, cb_ref, wa_ref, ba_ref, wx_ref, bx_ref, lam_ref,
               xbuf, hcar, y_out):
    ts, bw = yg.shape
    cs = slice(n * bw, (n + 1) * bw)
    gate = _gelu_tanh(yg)

    xbuf[SUBLANES:SUBLANES + ts, cs] = xr
    cw = cw_ref[:, cs]
    xc = cb_ref[:, cs] + xbuf[pl.ds(SUBLANES - 3, ts), cs] * cw[0:1]
    xc = xc + xbuf[pl.ds(SUBLANES - 2, ts), cs] * cw[1:2]
    xc = xc + xbuf[pl.ds(SUBLANES - 1, ts), cs] * cw[2:3]
    xc = xc + xr * cw[3:4]
    xbuf[0:SUBLANES, cs] = xbuf[ts:ts + SUBLANES, cs]

    xcb = xc.astype(BF16)
    r = _sigmoid(_dot(xcb, wa_ref[n]) + ba_ref[:, cs])
    ig = _sigmoid(_dot(xcb, wx_ref[n]) + bx_ref[:, cs])
    nl = -lam_ref[:, cs]
    softplus = jnp.maximum(nl, 0.0) + jnp.log1p(jnp.exp(-jnp.abs(nl)))
    log_a = (-RG_C * r) * softplus
    a = jnp.exp(log_a)
    v = -jnp.tanh(log_a) * (a * a + 1.0)
    root = jnp.where(v > 0.0, v * lax.rsqrt(v), 0.0)
    u = root * (ig * xc)

    row = lax.broadcasted_iota(jnp.int32, (SUBLANES, bw), 0)
    masks = [(s, row >= s) for s in (1, 2, 4)]
    hp = hcar[:, cs]
    hs = []
    for c in range(ts // SUBLANES):
        av = a[c * SUBLANES:(c + 1) * SUBLANES]
        uv = u[c * SUBLANES:(c + 1) * SUBLANES]
        for s, m in masks:
            a_sh = jnp.where(m, pltpu.roll(av, s, 0), 1.0)
            u_sh = jnp.where(m, pltpu.roll(uv, s, 0), 0.0)
            uv = av * u_sh + uv
            av = av * a_sh
        hv = av * hp + uv
        hs.append(hv)
        hp = hv[SUBLANES - 1:SUBLANES, :]
    hcar[:, cs] = hp
    y_out[:, cs] = (jnp.concatenate(hs, axis=0) * gate).astype(BF16)


def _rec_mlp_body(x_ref, xp_ref, g_ref, win_ref, cw_ref, cb_ref, wa_ref, ba_ref, wx_ref, bx_ref,
                  lam_ref, wpre_ref, gm_ref, wup_ref, wdn_ref, gfin_ref, o_ref,
                  xbuf, hcar, y_s, *, ts, tiles_per_seq, ff_chunk, final):
    t = pl.program_id(0)
    d = x_ref.shape[-1]
    d_ff = wup_ref.shape[1]
    n_chunks = d_ff // ff_chunk

    @pl.when(lax.rem(t, tiles_per_seq) == 0)
    def _():
        xbuf[0:SUBLANES, :] = jnp.zeros((SUBLANES, d), F32)
        hcar[...] = jnp.zeros((1, d), F32)

    @pl.when(t == 0)
    def _():
        y_s[...] = jnp.zeros(y_s.shape, BF16)

    y2 = _dot(_rms(x_ref[...], g_ref[...]).astype(BF16), win_ref[...])
    bw = d // RG_BLOCKS
    x1 = xp_ref[...] + _dot(y_s[...], wpre_ref[...])
    hn = _rms(x1, gm_ref[...]).astype(BF16)
    acc = x1
    hs = {}

    def up(c):
        h = jnp.maximum(_dot(hn, wup_ref[:, c * ff_chunk:(c + 1) * ff_chunk]), 0.0)
        hs[c] = (h * h).astype(BF16)

    def down(c):
        return _dot(hs.pop(c), wdn_ref[c * ff_chunk:(c + 1) * ff_chunk, :])

    mlp_ops = []
    for c in range(n_chunks):
        if c + 1 < n_chunks:
            mlp_ops.append((up, c + 1))
        mlp_ops.append((down, c))
    up(0)
    for n in range(RG_BLOCKS):
        _rec_block(n, y2[:, n * bw:(n + 1) * bw], y2[:, d + n * bw:d + (n + 1) * bw],
                   cw_ref, cb_ref, wa_ref, ba_ref, wx_ref, bx_ref, lam_ref, xbuf, hcar, y_s)
        take = -(-len(mlp_ops) // (RG_BLOCKS - n))
        for fn, c in mlp_ops[:take]:
            r = fn(c)
            if r is not None:
                acc = acc + r
        mlp_ops = mlp_ops[take:]
    if final:
        acc = _rms(acc, gfin_ref[...])
    o_ref[...] = acc


def _rec_mlp_layer(x, g, w_in, conv_w, conv_b, w_a, b_a, w_x, b_x, lam,
                   w_pre, g_mlp, w_up, w_dn, g_fin, *, layer, final, ts=256, ff_chunk=1024):
    b, s, d = x.shape
    ts = min(ts, s)
    bw = d // RG_BLOCKS
    d_ff = w_up.shape[-1]
    n_tiles = b * s // ts
    x2d = x.reshape(b * s, d)
    cur = lambda t: (jnp.minimum(t, n_tiles - 1), 0)
    prev = lambda t: (jnp.maximum(t - 1, 0), 0)
    out = pl.pallas_call(
        functools.partial(_rec_mlp_body, ts=ts, tiles_per_seq=s // ts,
                          ff_chunk=min(ff_chunk, d_ff), final=final),
        grid=(n_tiles + 1,),
        in_specs=[
            pl.BlockSpec((ts, d), cur),
            pl.BlockSpec((ts, d), prev),
            _const_spec((1, d)),
            _const_spec((d, 2 * d)),
            _const_spec((CONV_W, d)),
            _const_spec((1, d)),
            _const_spec((RG_BLOCKS, bw, bw)),
            _const_spec((1, d)),
            _const_spec((RG_BLOCKS, bw, bw)),
            _const_spec((1, d)),
            _const_spec((1, d)),
            _const_spec((d, d)),
            _const_spec((1, d)),
            _layer_spec((d, d_ff), layer),
            _layer_spec((d_ff, d), layer),
            _const_spec((1, d)),
        ],
        out_specs=pl.BlockSpec((ts, d), prev),
        out_shape=jax.ShapeDtypeStruct((b * s, d), F32),
        scratch_shapes=[
            pltpu.VMEM((ts + SUBLANES, d), F32),
            pltpu.VMEM((1, d), F32),
            pltpu.VMEM((ts, d), BF16),
        ],
        compiler_params=pltpu.CompilerParams(
            dimension_semantics=("arbitrary",), vmem_limit_bytes=VMEM_LIMIT_BYTES,
        ),
        name="rglru_mlp",
    )(x2d, x2d, g, w_in, conv_w, conv_b, w_a, b_a, w_x, b_x, lam, w_pre, g_mlp, w_up, w_dn, g_fin)
    return out.reshape(b, s, d)


def _attproj_body(x_ref, g_ref, w_ref, gkv_ref, wuk_ref,
                  qlat_ref, caug_ref, qi_ref, kia_ref, kib_ref, wi_ref):
    dq = H_ATT * D_QK
    di = H_IDX * D_IDX
    hn = _rms(x_ref[0], g_ref[...]).astype(BF16)
    y = _dot(hn, w_ref[...])
    ts = y.shape[0]
    o = dq
    craw = y[:, o:o + D_LAT]
    o += D_LAT
    qi_ref[0] = y[:, o:o + di].astype(BF16)
    o += di
    tail = y[:, o:o + LANES]
    moved = pltpu.roll(tail, D_IDX, 1)
    lane = lax.broadcasted_iota(jnp.int32, tail.shape, 1)
    kia_ref[0] = jnp.where(lane < D_IDX, tail, 0.0).astype(BF16)
    kib_ref[0] = jnp.where(lane >= D_IDX, moved, 0.0).astype(BF16)
    wi_ref[0] = jnp.where(lane < H_IDX, moved, 0.0) * IDX_W_SCALE

    c = _rms(craw, gkv_ref[...])
    caug_ref[0] = jnp.concatenate([c, jnp.ones((ts, D_LAT), F32)], axis=1).astype(BF16)

    for m in range(H_ATT // 2):
        qp = y[:, m * LANES:(m + 1) * LANES].astype(BF16)
        ql = _dot(qp, wuk_ref[m]) * (ATT_SCALE * LOG2E)
        qlat_ref[0, 2 * m] = ql[:, :D_LAT].astype(BF16)
        qlat_ref[0, 2 * m + 1] = ql[:, D_LAT:].astype(BF16)


def _att_proj(x, g, w_pack, g_kv, wuk2, *, ts=256):
    b, s, d = x.shape
    ts = min(ts, s)
    n_out = w_pack.shape[1]
    di = H_IDX * D_IDX
    row = lambda i, j: (i, j, 0)
    return pl.pallas_call(
        _attproj_body,
        grid=(b, s // ts),
        in_specs=[
            pl.BlockSpec((1, ts, d), row),
            _const_spec((1, d)),
            _const_spec((d, n_out)),
            _const_spec((1, D_LAT)),
            _const_spec((H_ATT // 2, 2 * D_QK, 2 * D_LAT)),
        ],
        out_specs=[
            pl.BlockSpec((1, H_ATT, ts, D_LAT), lambda i, j: (i, 0, j, 0)),
            pl.BlockSpec((1, ts, 2 * D_LAT), row),
            pl.BlockSpec((1, ts, di), row),
            pl.BlockSpec((1, ts, LANES), row),
            pl.BlockSpec((1, ts, LANES), row),
            pl.BlockSpec((1, ts, LANES), row),
        ],
        out_shape=[
            jax.ShapeDtypeStruct((b, H_ATT, s, D_LAT), BF16),
            jax.ShapeDtypeStruct((b, s, 2 * D_LAT), BF16),
            jax.ShapeDtypeStruct((b, s, di), BF16),
            jax.ShapeDtypeStruct((b, s, LANES), BF16),
            jax.ShapeDtypeStruct((b, s, LANES), BF16),
            jax.ShapeDtypeStruct((b, s, LANES), F32),
        ],
        compiler_params=pltpu.CompilerParams(
            dimension_semantics=("arbitrary", "arbitrary"), vmem_limit_bytes=VMEM_LIMIT_BYTES),
        name="att_proj",
    )(x, g, w_pack, g_kv, wuk2)


def _t5_bucket(dist):
    n = jnp.maximum(dist, 0)
    nf = jnp.maximum(n, 1).astype(F32)
    large = MAX_EXACT + jnp.floor(jnp.log(nf / MAX_EXACT) / math.log(MAX_DISTANCE / MAX_EXACT)
                                  * (NUM_BUCKETS - MAX_EXACT)).astype(jnp.int32)
    large = jnp.minimum(large, NUM_BUCKETS - 1)
    return jnp.where(n < MAX_EXACT, n, large)


def _attn_body(rb_ref, qlat_ref, qi_ref, wi_ref, caug_ref, kia_ref, kib_ref, wuv_ref, o_ref,
               sc_std, sc_t, acc, mrun, alph, btab, wib, p_s, tj, thr, jthr, *, topk, seq):
    tq, tk = Q_BLOCK, K_TILE
    nb = tk // tq
    step = pl.program_id(1)
    n_kt = step + 1
    hq = H_ATT * tq
    pairs = H_IDX // 2

    @pl.when((pl.program_id(0) == 0) & (step == 0))
    def _():
        ql = lax.broadcasted_iota(jnp.int32, (tq, tq), 0)
        kl = lax.broadcasted_iota(jnp.int32, (tq, tq), 1)
        for var in range(2):
            bucket = _t5_bucket(ql - kl + var * tq)
            for h in range(H_ATT):
                t = jnp.zeros((tq, tq), F32)
                for bk in range(NUM_BUCKETS - 1):
                    t = jnp.where(bucket == bk, rb_ref[bk, h] - rb_ref[NUM_BUCKETS - 1, h], t)
                btab[h, var] = t * LOG2E

    qi = qi_ref[0]
    qi8 = jnp.concatenate(
        [qi[blk * tq:(blk + 1) * tq, m * LANES:(m + 1) * LANES]
         for blk in range(nb) for m in range(pairs)], axis=0)
    wi = wi_ref[0]
    for blk in range(nb):
        for h in range(H_IDX):
            wib[blk, h] = jnp.broadcast_to(wi[blk * tq:(blk + 1) * tq, h:h + 1], (tq, tk))
    row_q = lax.broadcasted_iota(jnp.int32, (tq, tk), 0)
    lane_k = lax.broadcasted_iota(jnp.int32, (tq, tk), 1)

    def score_tile(kt, carry, *, last):
        k0 = pl.multiple_of(kt * tk, tk)
        da = jnp.maximum(_dot_nt(qi8, kia_ref[0, pl.ds(k0, tk), :]), 0.0)
        db = jnp.maximum(_dot_nt(qi8, kib_ref[0, pl.ds(k0, tk), :]), 0.0)
        for blk in range(nb):
            s = jnp.zeros((tq, tk), F32)
            for m in range(pairs):
                r0 = (blk * pairs + m) * tq
                s = s + wib[blk, 2 * m] * da[r0:r0 + tq]
                s = s + wib[blk, 2 * m + 1] * db[r0:r0 + tq]
            if last:
                s = jnp.where(lane_k <= row_q + blk * tq, s, NEG_INF)
            sc_std[blk, kt] = s
            sc_t[kt, :, blk * tq:(blk + 1) * tq] = s.T
        return carry

    def score_pair(j, carry):
        score_tile(2 * j, carry, last=False)
        return score_tile(2 * j + 1, carry, last=False)

    lax.fori_loop(0, lax.shift_right_logical(step, 1), score_pair, 0)

    @pl.when((step & 1) == 1)
    def _():
        score_tile(step - 1, 0, last=False)

    score_tile(step, 0, last=True)

    st = (SUBLANES, nb * tq)
    p_lane = step * (nb * tq) + lax.broadcasted_iota(jnp.int32, st, 1)
    kq = jnp.minimum(topk, p_lane + 1).astype(F32)

    def search(n):
        def over_tiles(fn, op):
            pair = {jnp.sum: jnp.add, jnp.max: jnp.maximum, jnp.min: jnp.minimum}[op]
            parts = [op(fn(sc_t[kt].reshape(tk // SUBLANES, SUBLANES, nb * tq), kt), axis=0)
                     for kt in range(n)]
            while len(parts) > 1:
                parts = [pair(*parts[i:i + 2]) if i + 1 < len(parts) else parts[i]
                         for i in range(0, len(parts), 2)]
            x = parts[0]
            for s in (4, 2, 1):
                x = pair(x, pltpu.roll(x, s, 0))
            return x

        def count_ge(th):
            return over_tiles(lambda v, kt: jnp.where(v >= th, 1.0, 0.0), jnp.sum)

        mx = over_tiles(lambda v, kt: v, jnp.max)
        lo0 = over_tiles(lambda v, kt: jnp.where(v == NEG_INF, POS_INF, v), jnp.min)

        def bisect(_, c):
            lo, hix, chi = c
            hib = jnp.where(hix == POS_INF, mx, hix)
            mid = 0.5 * lo + 0.5 * hib
            cnt = count_ge(mid)
            ge = cnt >= kq
            return jnp.where(ge, mid, lo), jnp.where(ge, hix, mid), jnp.where(ge, chi, cnt)

        _, hix, chi = lax.fori_loop(
            0, N_BISECT, bisect,
            (lo0, jnp.full(st, POS_INF, F32), jnp.zeros(st, F32)))

        def sweep(cand):
            cs, ms = [], []
            for kt in range(n):
                v = sc_t[kt].reshape(tk // SUBLANES, SUBLANES, nb * tq)
                ge = v >= cand
                cs.append(jnp.sum(jnp.where(ge, 1.0, 0.0), axis=0))
                ms.append(jnp.max(jnp.where(ge, NEG_INF, v), axis=0))
            c, m = functools.reduce(jnp.add, cs), functools.reduce(jnp.maximum, ms)
            for s in (4, 2, 1):
                c = c + pltpu.roll(c, s, 0)
                m = jnp.maximum(m, pltpu.roll(m, s, 0))
            return c, m

        def snap_cond(c):
            return jnp.max(c[5]) > 0.0

        def snap_body(c):
            cand, chi, t, cge, ngt, pending = c
            ct, nxt = sweep(cand)
            act = pending > 0.0
            ok = ct >= kq
            t = jnp.where(act, cand, t)
            cge = jnp.where(act, ct, cge)
            ngt = jnp.where(act, chi, ngt)
            pending = jnp.where(act & ok, 0.0, pending)
            cand = jnp.where(act & ~ok, nxt, cand)
            chi = jnp.where(act & ~ok, ct, chi)
            return cand, chi, t, cge, ngt, pending

        cand0 = over_tiles(lambda v, kt: jnp.where(v < hix, v, NEG_INF), jnp.max)
        zero = jnp.zeros(st, F32)
        first_sweep = snap_body((cand0, chi, zero, zero, zero, jnp.ones(st, F32)))
        _, _, t_k, cge, ngt, _ = lax.while_loop(snap_cond, snap_body, first_sweep)
        thr[...] = t_k
        jthr[...] = jnp.full(st, float(seq), F32)

        need = kq - ngt
        excess = (cge - ngt) > need

        @pl.when(jnp.max(jnp.where(excess, 1.0, 0.0)) > 0.0)
        def _():
            sub_k = lax.broadcasted_iota(jnp.int32, (tk, nb * tq), 0).astype(F32).reshape(
                tk // SUBLANES, SUBLANES, nb * tq)

            def jbisect(_, c):
                jlo, jhi = c
                mid = jnp.floor(0.5 * (jlo + jhi))
                cnt = over_tiles(
                    lambda v, kt: jnp.where((v == t_k) & (sub_k + float(kt * tk) <= mid), 1.0, 0.0),
                    jnp.sum)
                ge = cnt >= need
                return jnp.where(ge, jlo, mid), jnp.where(ge, mid, jhi)

            n_pass = max(1, int(math.ceil(math.log2(seq))))
            _, jhi = lax.fori_loop(
                0, n_pass, jbisect,
                (jnp.full(st, -1.0, F32), jnp.full(st, float(seq - 1), F32)))
            jthr[...] = jhi

    for n in range(1, seq // tk + 1):
        pl.when(n_kt == n)(functools.partial(search, n))

    for blk in range(nb):
        for i, ref in enumerate((thr, jthr)):
            v = jnp.concatenate([ref[:, blk * tq:(blk + 1) * tq]] * (tq // SUBLANES), axis=0).T
            tj[blk, i] = jnp.concatenate([v] * (tk // tq), axis=1)

    lane_kf = lane_k.astype(F32)

    def attend_tile(kt, carry=None, *, kind, first=False):
        qall = jnp.concatenate(
            [qlat_ref[0, :, blk * tq:(blk + 1) * tq, :].reshape(hq, D_LAT) for blk in range(nb)],
            axis=0)
        k0 = pl.multiple_of(kt * tk, tk)
        ct = caug_ref[0, pl.ds(k0, tk), :]
        src = _dot_nt(qall, ct[:, :D_LAT])
        kf = lane_kf + (kt * tk).astype(F32)
        for blk in range(nb):
            s = sc_std[blk, kt]
            tb = tj[blk, 0]
            sel = (s > tb) | ((s == tb) & (kf <= tj[blk, 1]))
            ma = jnp.where(sel, 0.0, NEG_INF)
            for h in range(H_ATT):
                r0 = (blk * H_ATT + h) * tq
                lh = src[r0:r0 + tq, :] + ma
                if kind == "near" and blk == 0:
                    lh = jnp.concatenate([lh[:, :tq], lh[:, tq:] + btab[h, 1]], axis=1)
                elif kind == "last" and blk == 0:
                    lh = jnp.concatenate([lh[:, :tq] + btab[h, 0], lh[:, tq:]], axis=1)
                elif kind == "last" and blk == 1:
                    lh = lh + jnp.concatenate([btab[h, 1], btab[h, 0]], axis=1)
                mt = jnp.maximum(lh[:, :tq], lh[:, tq:])
                rm = jnp.broadcast_to(jnp.max(mt, axis=1, keepdims=True), (tq, tq))
                if first:
                    m_new = jnp.maximum(rm, M_INIT)
                else:
                    m_old = mrun[blk, h]
                    m_new = jnp.maximum(m_old, rm)
                    alph[blk, h] = jnp.exp2(m_old - m_new)
                mrun[blk, h] = m_new
                m2 = jnp.concatenate([m_new, m_new], axis=1)
                p_s[r0:r0 + tq, :] = jnp.exp2(lh - m2).astype(BF16)
        pv = _dot(p_s[...], ct)
        if first:
            acc[...] = pv
            return carry
        for blk in range(nb):
            for h in range(H_ATT):
                r0 = (blk * H_ATT + h) * tq
                al = alph[blk, h]
                acc[r0:r0 + tq, :] = acc[r0:r0 + tq, :] * jnp.concatenate([al, al], axis=1) + pv[r0:r0 + tq]
        return carry

    attend_tile(step, kind="last", first=True)

    @pl.when(step >= 1)
    def _():
        attend_tile(step - 1, kind="near")

    lax.fori_loop(0, jnp.maximum(step - 1, 0), functools.partial(attend_tile, kind="far"), 0)

    for m in range(H_ATT // 2):
        rows = []
        for blk in range(nb):
            halves = []
            for h in (2 * m, 2 * m + 1):
                r0 = (blk * H_ATT + h) * tq
                a = acc[r0:r0 + tq, :]
                halves.append(a[:, :D_LAT] * (1.0 / a[:, D_LAT:]))
            rows.append(jnp.concatenate(halves, axis=1))
        pair = jnp.concatenate(rows, axis=0).astype(BF16)
        o_ref[0, :, m * 2 * D_V:(m + 1) * 2 * D_V] = _dot(pair, wuv_ref[m]).astype(BF16)


def _attention(rel_bias, qlat, qi, wi, caug, kia, kib, wuv2, *, topk):
    b, _, s, _ = qlat.shape
    tq, tk = Q_BLOCK, K_TILE
    assert s % tk == 0 and tk == 2 * tq
    nb = tk // tq
    n_kt = s // tk
    hq = H_ATT * tq
    di = H_IDX * D_IDX
    dv = H_ATT * D_V
    qrow = lambda i, j: (i, j, 0)
    full = lambda i, j: (i, 0, 0)
    return pl.pallas_call(
        functools.partial(_attn_body, topk=topk, seq=s),
        grid=(b, s // tk),
        in_specs=[
            pl.BlockSpec(memory_space=pltpu.SMEM),
            pl.BlockSpec((1, H_ATT, tk, D_LAT), lambda i, j: (i, 0, j, 0)),
            pl.BlockSpec((1, tk, di), qrow),
            pl.BlockSpec((1, tk, LANES), qrow),
            pl.BlockSpec((1, s, 2 * D_LAT), full),
            pl.BlockSpec((1, s, LANES), full),
            pl.BlockSpec((1, s, LANES), full),
            _const_spec((H_ATT // 2, 2 * D_LAT, 2 * D_V)),
        ],
        out_specs=pl.BlockSpec((1, tk, dv), qrow),
        out_shape=jax.ShapeDtypeStruct((b, s, dv), BF16),
        scratch_shapes=[
            pltpu.VMEM((nb, n_kt, tq, tk), F32),
            pltpu.VMEM((n_kt, tk, nb * tq), F32),
            pltpu.VMEM((nb * hq, tk), F32),
            pltpu.VMEM((nb, H_ATT, tq, tq), F32),
            pltpu.VMEM((nb, H_ATT, tq, tq), F32),
            pltpu.VMEM((H_ATT, 2, tq, tq), F32),
            pltpu.VMEM((nb, H_IDX, tq, tk), F32),
            pltpu.VMEM((nb * hq, tk), BF16),
            pltpu.VMEM((nb, 2, tq, tk), F32),
            pltpu.VMEM((SUBLANES, nb * tq), F32),
            pltpu.VMEM((SUBLANES, nb * tq), F32),
        ],
        compiler_params=pltpu.CompilerParams(
            dimension_semantics=("arbitrary", "arbitrary"), vmem_limit_bytes=VMEM_LIMIT_BYTES),
        name="dsa_attention",
    )(rel_bias, qlat, qi, wi, caug, kia, kib, wuv2)


def _block_diag_pairs(w):
    h, a, b = w.shape
    w = w.reshape(h // 2, 2, a, b)
    z = jnp.zeros((h // 2, a, b), w.dtype)
    top = jnp.concatenate([w[:, 0], z], axis=2)
    bot = jnp.concatenate([z, w[:, 1]], axis=2)
    return jnp.concatenate([top, bot], axis=1)


def _pack_att_w(w_in):
    assert 2 * D_IDX == LANES and D_IDX + H_IDX <= LANES
    n = w_in.shape[1]
    return jnp.pad(w_in, ((0, 0), (0, -n % LANES)))


def kernel(x, norm_mix_g, norm_mlp_g, final_norm_g, rec_w_in, rec_conv_w, rec_conv_b, rec_w_a, rec_b_a, rec_w_x, rec_b_x, rec_lambda, rec_w_out, att_w_in, att_kv_norm_g, att_w_uk, att_w_uv, att_w_o, rel_bias, mlp_w_up, mlp_w_down):
    b, s, d = x.shape
    depth = norm_mix_g.shape[0]
    topk = min(TOPK_MAX, s // 4)
    row = lambda v: v.reshape(1, -1).astype(F32)
    w_up_all = mlp_w_up.astype(BF16)
    w_dn_all = mlp_w_down.astype(BF16)
    for layer in range(depth):
        j = layer // 2
        g_mix = row(norm_mix_g[layer])
        final = layer == depth - 1
        mlp_w = (row(norm_mlp_g[layer]), w_up_all, w_dn_all, row(final_norm_g))
        if layer % 2 == 0:
            x = _rec_mlp_layer(
                x, g_mix, rec_w_in[j].astype(BF16), rec_conv_w[j].astype(F32), row(rec_conv_b[j]),
                rec_w_a[j].astype(BF16), row(rec_b_a[j]), rec_w_x[j].astype(BF16), row(rec_b_x[j]),
                row(rec_lambda[j]), rec_w_out[j].astype(BF16), *mlp_w, layer=layer, final=final)
        else:
            qlat, caug, qi, kia, kib, wi = _att_proj(
                x, g_mix, _pack_att_w(att_w_in[j].astype(BF16)), row(att_kv_norm_g[j]),
                _block_diag_pairs(att_w_uk[j]).astype(BF16))
            y = _attention(
                rel_bias.astype(F32), qlat, qi, wi, caug, kia, kib,
                _block_diag_pairs(att_w_uv[j]).astype(BF16), topk=topk)
            x = _mlp_layer(
                x.reshape(b * s, d), y.reshape(b * s, d), att_w_o[j].astype(BF16), *mlp_w,
                layer=layer, final=final).reshape(b, s, d)
    return x
```

```python
import functools
import math

import jax
import jax.numpy as jnp
from jax import lax
from jax.experimental import pallas as pl
from jax.experimental.pallas import tpu as pltpu

F32 = jnp.float32
BF16 = jnp.bfloat16

EPS = 1e-6
RG_BLOCKS = 4
CONV_W = 4
RG_C = 8.0
H_ATT = 16
D_QK = 64
D_V = 64
D_LAT = 128
H_IDX = 8
D_IDX = 64
TOPK_MAX = 256
ATT_SCALE = D_QK ** -0.5
LOG2E = math.log2(math.e)
IDX_W_SCALE = (H_IDX ** -0.5) * (D_IDX ** -0.5)
NUM_BUCKETS = 32
MAX_DISTANCE = 128
MAX_EXACT = NUM_BUCKETS // 2

LANES = 128
SUBLANES = 8
VMEM_LIMIT_BYTES = 56 * 1024 * 1024

Q_BLOCK = 128
K_TILE = 256
N_BISECT = 14
NEG_INF = float("-inf")
POS_INF = float("inf")
M_INIT = -1e30


def _rms(x, g):
    ms = jnp.mean(x * x, axis=-1, keepdims=True)
    return x * lax.rsqrt(ms + EPS) * g


def _dot(a, b):
    return jnp.dot(a, b, preferred_element_type=F32)


def _dot_nt(a, b):
    return lax.dot_general(a, b, (((1,), (1,)), ((), ())), preferred_element_type=F32)


def _const_spec(shape):
    nd = len(shape)
    return pl.BlockSpec(shape, lambda *_: (0,) * nd, pipeline_mode=pl.Buffered(1))


def _layer_spec(shape, layer):
    nd = len(shape)
    return pl.BlockSpec((None, *shape), lambda *_: (layer,) + (0,) * nd,
                        pipeline_mode=pl.Buffered(1))


def _mlp_tile(x, y, wpre_ref, g_ref, wup_ref, wdn_ref, gfin_ref, *, ff_chunk, final):
    x1 = x + _dot(y, wpre_ref[...])
    hn = _rms(x1, g_ref[...]).astype(BF16)
    acc = x1
    d_ff = wup_ref.shape[1]
    for c in range(d_ff // ff_chunk):
        h = _dot(hn, wup_ref[:, c * ff_chunk:(c + 1) * ff_chunk])
        h = jnp.maximum(h, 0.0)
        h = (h * h).astype(BF16)
        acc = acc + _dot(h, wdn_ref[c * ff_chunk:(c + 1) * ff_chunk, :])
    if final:
        acc = _rms(acc, gfin_ref[...])
    return acc


def _mlp_body(x_ref, y_ref, wpre_ref, g_ref, wup_ref, wdn_ref, gfin_ref, o_ref, *, ff_chunk, final):
    o_ref[...] = _mlp_tile(x_ref[...], y_ref[...], wpre_ref, g_ref, wup_ref, wdn_ref, gfin_ref,
                           ff_chunk=ff_chunk, final=final)


def _mlp_layer(x2d, y2d, w_pre, g, w_up, w_dn, g_fin, *, layer, final, tm=512, ff_chunk=512):
    n, d = x2d.shape
    d_ff = w_up.shape[-1]
    tm = min(tm, n)
    return pl.pallas_call(
        functools.partial(_mlp_body, ff_chunk=min(ff_chunk, d_ff), final=final),
        grid=(n // tm,),
        in_specs=[
            pl.BlockSpec((tm, d), lambda i: (i, 0)),
            pl.BlockSpec((tm, d), lambda i: (i, 0)),
            _const_spec((d, d)),
            _const_spec((1, d)),
            _layer_spec((d, d_ff), layer),
            _layer_spec((d_ff, d), layer),
            _const_spec((1, d)),
        ],
        out_specs=pl.BlockSpec((tm, d), lambda i: (i, 0)),
        out_shape=jax.ShapeDtypeStruct((n, d), F32),
        compiler_params=pltpu.CompilerParams(
            dimension_semantics=("arbitrary",), vmem_limit_bytes=VMEM_LIMIT_BYTES),
        name="mlp_block",
    )(x2d, y2d, w_pre, g, w_up, w_dn, g_fin)


def _gelu_tanh(x):
    c = math.sqrt(2.0 / math.pi)
    return 0.5 * x * (1.0 + jnp.tanh(c * (x + 0.044715 * (x * x * x))))


def _sigmoid(x):
    return 1.0 / (1.0 + jnp.exp(-x))


def _rec_block(n, yg, xr, cw_ref, cb_ref, wa_ref, ba_ref, wx_ref, bx_ref, lam_ref,
               xbuf, hcar, y_out):
    ts, bw = yg.shape
    cs = slice(n * bw, (n + 1) * bw)
    gate = _gelu_tanh(yg)

    xbuf[SUBLANES:SUBLANES + ts, cs] = xr
    cw = cw_ref[:, cs]
    xc = cb_ref[:, cs] + xbuf[pl.ds(SUBLANES - 3, ts), cs] * cw[0:1]
    xc = xc + xbuf[pl.ds(SUBLANES - 2, ts), cs] * cw[1:2]
    xc = xc + xbuf[pl.ds(SUBLANES - 1, ts), cs] * cw[2:3]
    xc = xc + xr * cw[3:4]
    xbuf[0:SUBLANES, cs] = xbuf[ts:ts + SUBLANES, cs]

    xcb = xc.astype(BF16)
    r = _sigmoid(_dot(xcb, wa_ref[n]) + ba_ref[:, cs])
    ig = _sigmoid(_dot(xcb, wx_ref[n]) + bx_ref[:, cs])
    nl = -lam_ref[:, cs]
    softplus = jnp.maximum(nl, 0.0) + jnp.log1p(jnp.exp(-jnp.abs(nl)))
    log_a = (-RG_C * r) * softplus
    a = jnp.exp(log_a)
    v = -jnp.tanh(log_a) * (a * a + 1.0)
    root = jnp.where(v > 0.0, v * lax.rsqrt(v), 0.0)
    u = root * (ig * xc)

    row = lax.broadcasted_iota(jnp.int32, (SUBLANES, bw), 0)
    masks = [(s, row >= s) for s in (1, 2, 4)]
    hp = hcar[:, cs]
    hs = []
    for c in range(ts // SUBLANES):
        av = a[c * SUBLANES:(c + 1) * SUBLANES]
        uv = u[c * SUBLANES:(c + 1) * SUBLANES]
        for s, m in masks:
            a_sh = jnp.where(m, pltpu.roll(av, s, 0), 1.0)
            u_sh = jnp.where(m, pltpu.roll(uv, s, 0), 0.0)
            uv = av * u_sh + uv
            av = av * a_sh
        hv = av * hp + uv
        hs.append(hv)
        hp = hv[SUBLANES - 1:SUBLANES, :]
    hcar[:, cs] = hp
    y_out[:, cs] = (jnp.concatenate(hs, axis=0) * gate).astype(BF16)


def _rec_mlp_body(x_ref, xp_ref, g_ref, win_ref, cw_ref, cb_ref, wa_ref, ba_ref, wx_ref, bx_ref,
                  lam_ref, wpre_ref, gm_ref, wup_ref, wdn_ref, gfin_ref, o_ref,
                  xbuf, hcar, y_s, *, ts, tiles_per_seq, ff_chunk, final):
    t = pl.program_id(0)
    d = x_ref.shape[-1]
    d_ff = wup_ref.shape[1]
    n_chunks = d_ff // ff_chunk

    @pl.when(lax.rem(t, tiles_per_seq) == 0)
    def _():
        xbuf[0:SUBLANES, :] = jnp.zeros((SUBLANES, d), F32)
        hcar[...] = jnp.zeros((1, d), F32)

    @pl.when(t == 0)
    def _():
        y_s[...] = jnp.zeros(y_s.shape, BF16)

    y2 = _dot(_rms(x_ref[...], g_ref[...]).astype(BF16), win_ref[...])
    bw = d // RG_BLOCKS
    x1 = xp_ref[...] + _dot(y_s[...], wpre_ref[...])
    hn = _rms(x1, gm_ref[...]).astype(BF16)
    acc = x1
    hs = {}

    def up(c):
        h = jnp.maximum(_dot(hn, wup_ref[:, c * ff_chunk:(c + 1) * ff_chunk]), 0.0)
        hs[c] = (h * h).astype(BF16)

    def down(c):
        return _dot(hs.pop(c), wdn_ref[c * ff_chunk:(c + 1) * ff_chunk, :])

    mlp_ops = []
    for c in range(n_chunks):
        if c + 1 < n_chunks:
            mlp_ops.append((up, c + 1))
        mlp_ops.append((down, c))
    up(0)
    for n in range(RG_BLOCKS):
        _rec_block(n, y2[:, n * bw:(n + 1) * bw], y2[:, d + n * bw:d + (n + 1) * bw],
                   cw_ref, cb_ref, wa_ref, ba_ref, wx_ref, bx_ref, lam_ref, xbuf, hcar, y_s)
        take = -(-len(mlp_ops) // (RG_BLOCKS - n))
        for fn, c in mlp_ops[:take]:
            r = fn(c)
            if r is not None:
                acc = acc + r
        mlp_ops = mlp_ops[take:]
    if final:
        acc = _rms(acc, gfin_ref[...])
    o_ref[...] = acc


def _rec_mlp_layer(x, g, w_in, conv_w, conv_b, w_a, b_a, w_x, b_x, lam,
                   w_pre, g_mlp, w_up, w_dn, g_fin, *, layer, final, ts=256, ff_chunk=1024):
    b, s, d = x.shape
    ts = min(ts, s)
    bw = d // RG_BLOCKS
    d_ff = w_up.shape[-1]
    n_tiles = b * s // ts
    x2d = x.reshape(b * s, d)
    cur = lambda t: (jnp.minimum(t, n_tiles - 1), 0)
    prev = lambda t: (jnp.maximum(t - 1, 0), 0)
    out = pl.pallas_call(
        functools.partial(_rec_mlp_body, ts=ts, tiles_per_seq=s // ts,
                          ff_chunk=min(ff_chunk, d_ff), final=final),
        grid=(n_tiles + 1,),
        in_specs=[
            pl.BlockSpec((ts, d), cur),
            pl.BlockSpec((ts, d), prev),
            _const_spec((1, d)),
            _const_spec((d, 2 * d)),
            _const_spec((CONV_W, d)),
            _const_spec((1, d)),
            _const_spec((RG_BLOCKS, bw, bw)),
            _const_spec((1, d)),
            _const_spec((RG_BLOCKS, bw, bw)),
            _const_spec((1, d)),
            _const_spec((1, d)),
            _const_spec((d, d)),
            _const_spec((1, d)),
            _layer_spec((d, d_ff), layer),
            _layer_spec((d_ff, d), layer),
            _const_spec((1, d)),
        ],
        out_specs=pl.BlockSpec((ts, d), prev),
        out_shape=jax.ShapeDtypeStruct((b * s, d), F32),
        scratch_shapes=[
            pltpu.VMEM((ts + SUBLANES, d), F32),
            pltpu.VMEM((1, d), F32),
            pltpu.VMEM((ts, d), BF16),
        ],
        compiler_params=pltpu.CompilerParams(
            dimension_semantics=("arbitrary",), vmem_limit_bytes=VMEM_LIMIT_BYTES,
        ),
        name="rglru_mlp",
    )(x2d, x2d, g, w_in, conv_w, conv_b, w_a, b_a, w_x, b_x, lam, w_pre, g_mlp, w_up, w_dn, g_fin)
    return out.reshape(b, s, d)


def _attproj_body(x_ref, g_ref, w_ref, gkv_ref, wuk_ref,
                  qlat_ref, caug_ref, qi_ref, kia_ref, kib_ref, wi_ref):
    dq = H_ATT * D_QK
    di = H_IDX * D_IDX
    hn = _rms(x_ref[0], g_ref[...]).astype(BF16)
    y = _dot(hn, w_ref[...])
    ts = y.shape[0]
    o = dq
    craw = y[:, o:o + D_LAT]
    o += D_LAT
    qi_ref[0] = y[:, o:o + di].astype(BF16)
    o += di
    tail = y[:, o:o + LANES]
    moved = pltpu.roll(tail, D_IDX, 1)
    lane = lax.broadcasted_iota(jnp.int32, tail.shape, 1)
    kia_ref[0] = jnp.where(lane < D_IDX, tail, 0.0).astype(BF16)
    kib_ref[0] = jnp.where(lane >= D_IDX, moved, 0.0).astype(BF16)
    wi_ref[0] = jnp.where(lane < H_IDX, moved, 0.0) * IDX_W_SCALE

    c = _rms(craw, gkv_ref[...])
    caug_ref[0] = jnp.concatenate([c, jnp.ones((ts, D_LAT), F32)], axis=1).astype(BF16)

    for m in range(H_ATT // 2):
        qp = y[:, m * LANES:(m + 1) * LANES].astype(BF16)
        ql = _dot(qp, wuk_ref[m]) * (ATT_SCALE * LOG2E)
        qlat_ref[0, 2 * m] = ql[:, :D_LAT].astype(BF16)
        qlat_ref[0, 2 * m + 1] = ql[:, D_LAT:].astype(BF16)


def _att_proj(x, g, w_pack, g_kv, wuk2, *, ts=512):
    b, s, d = x.shape
    ts = min(ts, s)
    n_out = w_pack.shape[1]
    di = H_IDX * D_IDX
    row = lambda i, j: (i, j, 0)
    return pl.pallas_call(
        _attproj_body,
        grid=(b, s // ts),
        in_specs=[
            pl.BlockSpec((1, ts, d), row),
            _const_spec((1, d)),
            _const_spec((d, n_out)),
            _const_spec((1, D_LAT)),
            _const_spec((H_ATT // 2, 2 * D_QK, 2 * D_LAT)),
        ],
        out_specs=[
            pl.BlockSpec((1, H_ATT, ts, D_LAT), lambda i, j: (i, 0, j, 0)),
            pl.BlockSpec((1, ts, 2 * D_LAT), row),
            pl.BlockSpec((1, ts, di), row),
            pl.BlockSpec((1, ts, LANES), row),
            pl.BlockSpec((1, ts, LANES), row),
            pl.BlockSpec((1, ts, LANES), row),
        ],
        out_shape=[
            jax.ShapeDtypeStruct((b, H_ATT, s, D_LAT), BF16),
            jax.ShapeDtypeStruct((b, s, 2 * D_LAT), BF16),
            jax.ShapeDtypeStruct((b, s, di), BF16),
            jax.ShapeDtypeStruct((b, s, LANES), BF16),
            jax.ShapeDtypeStruct((b, s, LANES), BF16),
            jax.ShapeDtypeStruct((b, s, LANES), F32),
        ],
        compiler_params=pltpu.CompilerParams(
            dimension_semantics=("arbitrary", "arbitrary"), vmem_limit_bytes=VMEM_LIMIT_BYTES),
        name="att_proj",
    )(x, g, w_pack, g_kv, wuk2)


def _t5_bucket(dist):
    n = jnp.maximum(dist, 0)
    nf = jnp.maximum(n, 1).astype(F32)
    large = MAX_EXACT + jnp.floor(jnp.log(nf / MAX_EXACT) / math.log(MAX_DISTANCE / MAX_EXACT)
                                  * (NUM_BUCKETS - MAX_EXACT)).astype(jnp.int32)
    large = jnp.minimum(large, NUM_BUCKETS - 1)
    return jnp.where(n < MAX_EXACT, n, large)


def _attn_body(rb_ref, qlat_ref, qi_ref, wi_ref, caug_ref, kia_ref, kib_ref, wuv_ref, o_ref,
               sc_std, sc_t, acc, mrun, alph, btab, wib, p_s, tj, thr, jthr, *, topk, seq):
    tq, tk = Q_BLOCK, K_TILE
    nb = tk // tq
    step = pl.program_id(1)
    n_kt = step + 1
    hq = H_ATT * tq
    pairs = H_IDX // 2

    @pl.when((pl.program_id(0) == 0) & (step == 0))
    def _():
        ql = lax.broadcasted_iota(jnp.int32, (tq, tq), 0)
        kl = lax.broadcasted_iota(jnp.int32, (tq, tq), 1)
        for var in range(2):
            bucket = _t5_bucket(ql - kl + var * tq)
            for h in range(H_ATT):
                t = jnp.zeros((tq, tq), F32)
                for bk in range(NUM_BUCKETS - 1):
                    t = jnp.where(bucket == bk, rb_ref[bk, h] - rb_ref[NUM_BUCKETS - 1, h], t)
                btab[h, var] = t * LOG2E

    qi = qi_ref[0]
    qi8 = jnp.concatenate(
        [qi[blk * tq:(blk + 1) * tq, m * LANES:(m + 1) * LANES]
         for blk in range(nb) for m in range(pairs)], axis=0)
    wi = wi_ref[0]
    for blk in range(nb):
        for h in range(H_IDX):
            wib[blk, h] = jnp.broadcast_to(wi[blk * tq:(blk + 1) * tq, h:h + 1], (tq, tk))
    row_q = lax.broadcasted_iota(jnp.int32, (tq, tk), 0)
    lane_k = lax.broadcasted_iota(jnp.int32, (tq, tk), 1)

    def score_tile(kt, carry, *, last):
        k0 = pl.multiple_of(kt * tk, tk)
        da = jnp.maximum(_dot_nt(qi8, kia_ref[0, pl.ds(k0, tk), :]), 0.0)
        db = jnp.maximum(_dot_nt(qi8, kib_ref[0, pl.ds(k0, tk), :]), 0.0)
        for blk in range(nb):
            s = jnp.zeros((tq, tk), F32)
            for m in range(pairs):
                r0 = (blk * pairs + m) * tq
                s = s + wib[blk, 2 * m] * da[r0:r0 + tq]
                s = s + wib[blk, 2 * m + 1] * db[r0:r0 + tq]
            if last:
                s = jnp.where(lane_k <= row_q + blk * tq, s, NEG_INF)
            sc_std[blk, kt] = s
            sc_t[kt, :, blk * tq:(blk + 1) * tq] = s.T
        return carry

    def score_pair(j, carry):
        score_tile(2 * j, carry, last=False)
        return score_tile(2 * j + 1, carry, last=False)

    lax.fori_loop(0, lax.shift_right_logical(step, 1), score_pair, 0)

    @pl.when((step & 1) == 1)
    def _():
        score_tile(step - 1, 0, last=False)

    score_tile(step, 0, last=True)

    st = (SUBLANES, nb * tq)
    p_lane = step * (nb * tq) + lax.broadcasted_iota(jnp.int32, st, 1)
    kq = jnp.minimum(topk, p_lane + 1).astype(F32)

    def search(n):
        def over_tiles(fn, op):
            pair = {jnp.sum: jnp.add, jnp.max: jnp.maximum, jnp.min: jnp.minimum}[op]
            parts = [op(fn(sc_t[kt].reshape(tk // SUBLANES, SUBLANES, nb * tq), kt), axis=0)
                     for kt in range(n)]
            while len(parts) > 1:
                parts = [pair(*parts[i:i + 2]) if i + 1 < len(parts) else parts[i]
                         for i in range(0, len(parts), 2)]
            x = parts[0]
            for s in (4, 2, 1):
                x = pair(x, pltpu.roll(x, s, 0))
            return x

        def count_ge(th):
            return over_tiles(lambda v, kt: jnp.where(v >= th, 1.0, 0.0), jnp.sum)

        mx = over_tiles(lambda v, kt: v, jnp.max)
        lo0 = over_tiles(lambda v, kt: jnp.where(v == NEG_INF, POS_INF, v), jnp.min)

        def bisect(_, c):
            lo, hix, chi = c
            hib = jnp.where(hix == POS_INF, mx, hix)
            mid = 0.5 * lo + 0.5 * hib
            cnt = count_ge(mid)
            ge = cnt >= kq
            return jnp.where(ge, mid, lo), jnp.where(ge, hix, mid), jnp.where(ge, chi, cnt)

        _, hix, chi = lax.fori_loop(
            0, N_BISECT, bisect,
            (lo0, jnp.full(st, POS_INF, F32), jnp.zeros(st, F32)))

        def sweep(cand):
            cs, ms = [], []
            for kt in range(n):
                v = sc_t[kt].reshape(tk // SUBLANES, SUBLANES, nb * tq)
                ge = v >= cand
                cs.append(jnp.sum(jnp.where(ge, 1.0, 0.0), axis=0))
                ms.append(jnp.max(jnp.where(ge, NEG_INF, v), axis=0))
            c, m = functools.reduce(jnp.add, cs), functools.reduce(jnp.maximum, ms)
            for s in (4, 2, 1):
                c = c + pltpu.roll(c, s, 0)
                m = jnp.maximum(m, pltpu.roll(m, s, 0))
            return c, m

        def snap_cond(c):
            return jnp.max(c[5]) > 0.0

        def snap_body(c):
            cand, chi, t, cge, ngt, pending = c
            ct, nxt = sweep(cand)
            act = pending > 0.0
            ok = ct >= kq
            t = jnp.where(act, cand, t)
            cge = jnp.where(act, ct, cge)
            ngt = jnp.where(act, chi, ngt)
            pending = jnp.where(act & ok, 0.0, pending)
            cand = jnp.where(act & ~ok, nxt, cand)
            chi = jnp.where(act & ~ok, ct, chi)
            return cand, chi, t, cge, ngt, pending

        cand0 = over_tiles(lambda v, kt: jnp.where(v < hix, v, NEG_INF), jnp.max)
        zero = jnp.zeros(st, F32)
        first_sweep = snap_body((cand0, chi, zero, zero, zero, jnp.ones(st, F32)))
        _, _, t_k, cge, ngt, _ = lax.while_loop(snap_cond, snap_body, first_sweep)
        thr[...] = t_k
        jthr[...] = jnp.full(st, float(seq), F32)

        need = kq - ngt
        excess = (cge - ngt) > need

        @pl.when(jnp.max(jnp.where(excess, 1.0, 0.0)) > 0.0)
        def _():
            sub_k = lax.broadcasted_iota(jnp.int32, (tk, nb * tq), 0).astype(F32).reshape(
                tk // SUBLANES, SUBLANES, nb * tq)

            def jbisect(_, c):
                jlo, jhi = c
                mid = jnp.floor(0.5 * (jlo + jhi))
                cnt = over_tiles(
                    lambda v, kt: jnp.where((v == t_k) & (sub_k + float(kt * tk) <= mid), 1.0, 0.0),
                    jnp.sum)
                ge = cnt >= need
                return jnp.where(ge, jlo, mid), jnp.where(ge, mid, jhi)

            n_pass = max(1, int(math.ceil(math.log2(seq))))
            _, jhi = lax.fori_loop(
                0, n_pass, jbisect,
                (jnp.full(st, -1.0, F32), jnp.full(st, float(seq - 1), F32)))
            jthr[...] = jhi

    for n in range(1, seq // tk + 1):
        pl.when(n_kt == n)(functools.partial(search, n))

    for blk in range(nb):
        for i, ref in enumerate((thr, jthr)):
            v = jnp.concatenate([ref[:, blk * tq:(blk + 1) * tq]] * (tq // SUBLANES), axis=0).T
            tj[blk, i] = jnp.concatenate([v] * (tk // tq), axis=1)

    lane_kf = lane_k.astype(F32)

    def attend_tile(kt, carry=None, *, kind, first=False):
        qall = jnp.concatenate(
            [qlat_ref[0, :, blk * tq:(blk + 1) * tq, :].reshape(hq, D_LAT) for blk in range(nb)],
            axis=0)
        k0 = pl.multiple_of(kt * tk, tk)
        ct = caug_ref[0, pl.ds(k0, tk), :]
        src = _dot_nt(qall, ct[:, :D_LAT])
        kf = lane_kf + (kt * tk).astype(F32)
        for blk in range(nb):
            s = sc_std[blk, kt]
            tb = tj[blk, 0]
            sel = (s > tb) | ((s == tb) & (kf <= tj[blk, 1]))
            ma = jnp.where(sel, 0.0, NEG_INF)
            for h in range(H_ATT):
                r0 = (blk * H_ATT + h) * tq
                lh = src[r0:r0 + tq, :] + ma
                if kind == "near" and blk == 0:
                    lh = jnp.concatenate([lh[:, :tq], lh[:, tq:] + btab[h, 1]], axis=1)
                elif kind == "last" and blk == 0:
                    lh = jnp.concatenate([lh[:, :tq] + btab[h, 0], lh[:, tq:]], axis=1)
                elif kind == "last" and blk == 1:
                    lh = lh + jnp.concatenate([btab[h, 1], btab[h, 0]], axis=1)
                mt = jnp.maximum(lh[:, :tq], lh[:, tq:])
                rm = jnp.broadcast_to(jnp.max(mt, axis=1, keepdims=True), (tq, tq))
                if first:
                    m_new = jnp.maximum(rm, M_INIT)
                else:
                    m_old = mrun[blk, h]
                    m_new = jnp.maximum(m_old, rm)
                    alph[blk, h] = jnp.exp2(m_old - m_new)
                mrun[blk, h] = m_new
                m2 = jnp.concatenate([m_new, m_new], axis=1)
                p_s[r0:r0 + tq, :] = jnp.exp2(lh - m2).astype(BF16)
        pv = _dot(p_s[...], ct)
        if first:
            acc[...] = pv
            return carry
        for blk in range(nb):
            for h in range(H_ATT):
                r0 = (blk * H_ATT + h) * tq
                al = alph[blk, h]
                acc[r0:r0 + tq, :] = acc[r0:r0 + tq, :] * jnp.concatenate([al, al], axis=1) + pv[r0:r0 + tq]
        return carry

    attend_tile(step, kind="last", first=True)

    @pl.when(step >= 1)
    def _():
        attend_tile(step - 1, kind="near")

    lax.fori_loop(0, jnp.maximum(step - 1, 0), functools.partial(attend_tile, kind="far"), 0)

    for m in range(H_ATT // 2):
        rows = []
        for blk in range(nb):
            halves = []
            for h in (2 * m, 2 * m + 1):
                r0 = (blk * H_ATT + h) * tq
                a = acc[r0:r0 + tq, :]
                halves.append(a[:, :D_LAT] * (1.0 / a[:, D_LAT:]))
            rows.append(jnp.concatenate(halves, axis=1))
        pair = jnp.concatenate(rows, axis=0).astype(BF16)
        o_ref[0, :, m * 2 * D_V:(m + 1) * 2 * D_V] = _dot(pair, wuv_ref[m]).astype(BF16)


def _attention(rel_bias, qlat, qi, wi, caug, kia, kib, wuv2, *, topk):
    b, _, s, _ = qlat.shape
    tq, tk = Q_BLOCK, K_TILE
    assert s % tk == 0 and tk == 2 * tq
    nb = tk // tq
    n_kt = s // tk
    hq = H_ATT * tq
    di = H_IDX * D_IDX
    dv = H_ATT * D_V
    qrow = lambda i, j: (i, j, 0)
    full = lambda i, j: (i, 0, 0)
    return pl.pallas_call(
        functools.partial(_attn_body, topk=topk, seq=s),
        grid=(b, s // tk),
        in_specs=[
            pl.BlockSpec(memory_space=pltpu.SMEM),
            pl.BlockSpec((1, H_ATT, tk, D_LAT), lambda i, j: (i, 0, j, 0)),
            pl.BlockSpec((1, tk, di), qrow),
            pl.BlockSpec((1, tk, LANES), qrow),
            pl.BlockSpec((1, s, 2 * D_LAT), full),
            pl.BlockSpec((1, s, LANES), full),
            pl.BlockSpec((1, s, LANES), full),
            _const_spec((H_ATT // 2, 2 * D_LAT, 2 * D_V)),
        ],
        out_specs=pl.BlockSpec((1, tk, dv), qrow),
        out_shape=jax.ShapeDtypeStruct((b, s, dv), BF16),
        scratch_shapes=[
            pltpu.VMEM((nb, n_kt, tq, tk), F32),
            pltpu.VMEM((n_kt, tk, nb * tq), F32),
            pltpu.VMEM((nb * hq, tk), F32),
            pltpu.VMEM((nb, H_ATT, tq, tq), F32),
            pltpu.VMEM((nb, H_ATT, tq, tq), F32),
            pltpu.VMEM((H_ATT, 2, tq, tq), F32),
            pltpu.VMEM((nb, H_IDX, tq, tk), F32),
            pltpu.VMEM((nb * hq, tk), BF16),
            pltpu.VMEM((nb, 2, tq, tk), F32),
            pltpu.VMEM((SUBLANES, nb * tq), F32),
            pltpu.VMEM((SUBLANES, nb * tq), F32),
        ],
        compiler_params=pltpu.CompilerParams(
            dimension_semantics=("arbitrary", "arbitrary"), vmem_limit_bytes=VMEM_LIMIT_BYTES),
        name="dsa_attention",
    )(rel_bias, qlat, qi, wi, caug, kia, kib, wuv2)


def _block_diag_pairs(w):
    h, a, b = w.shape
    w = w.reshape(h // 2, 2, a, b)
    z = jnp.zeros((h // 2, a, b), w.dtype)
    top = jnp.concatenate([w[:, 0], z], axis=2)
    bot = jnp.concatenate([z, w[:, 1]], axis=2)
    return jnp.concatenate([top, bot], axis=1)


def _pack_att_w(w_in):
    assert 2 * D_IDX == LANES and D_IDX + H_IDX <= LANES
    n = w_in.shape[1]
    return jnp.pad(w_in, ((0, 0), (0, -n % LANES)))


def kernel(x, norm_mix_g, norm_mlp_g, final_norm_g, rec_w_in, rec_conv_w, rec_conv_b, rec_w_a, rec_b_a, rec_w_x, rec_b_x, rec_lambda, rec_w_out, att_w_in, att_kv_norm_g, att_w_uk, att_w_uv, att_w_o, rel_bias, mlp_w_up, mlp_w_down):
    b, s, d = x.shape
    depth = norm_mix_g.shape[0]
    topk = min(TOPK_MAX, s // 4)
    row = lambda v: v.reshape(1, -1).astype(F32)
    w_up_all = mlp_w_up.astype(BF16)
    w_dn_all = mlp_w_down.astype(BF16)
    for layer in range(depth):
        j = layer // 2
        g_mix = row(norm_mix_g[layer])
        final = layer == depth - 1
        mlp_w = (row(norm_mlp_g[layer]), w_up_all, w_dn_all, row(final_norm_g))
        if layer % 2 == 0:
            x = _rec_mlp_layer(
                x, g_mix, rec_w_in[j].astype(BF16), rec_conv_w[j].astype(F32), row(rec_conv_b[j]),
                rec_w_a[j].astype(BF16), row(rec_b_a[j]), rec_w_x[j].astype(BF16), row(rec_b_x[j]),
                row(rec_lambda[j]), rec_w_out[j].astype(BF16), *mlp_w, layer=layer, final=final)
        else:
            qlat, caug, qi, kia, kib, wi = _att_proj(
                x, g_mix, _pack_att_w(att_w_in[j].astype(BF16)), row(att_kv_norm_g[j]),
                _block_diag_pairs(att_w_uk[j]).astype(BF16))
            y = _attention(
                rel_bias.astype(F32), qlat, qi, wi, caug, kia, kib,
                _block_diag_pairs(att_w_uv[j]).astype(BF16), topk=topk)
            x = _mlp_layer(
                x.reshape(b * s, d), y.reshape(b * s, d), att_w_o[j].astype(BF16), *mlp_w,
                layer=layer, final=final).reshape(b, s, d)
    return x
```

```python
import functools
import math

import jax
import jax.numpy as jnp
from jax import lax
from jax.experimental import pallas as pl
from jax.experimental.pallas import tpu as pltpu

F32 = jnp.float32
BF16 = jnp.bfloat16

EPS = 1e-6
RG_BLOCKS = 4
CONV_W = 4
RG_C = 8.0
H_ATT = 16
D_QK = 64
D_V = 64
D_LAT = 128
H_IDX = 8
D_IDX = 64
TOPK_MAX = 256
ATT_SCALE = D_QK ** -0.5
LOG2E = math.log2(math.e)
IDX_W_SCALE = (H_IDX ** -0.5) * (D_IDX ** -0.5)
NUM_BUCKETS = 32
MAX_DISTANCE = 128
MAX_EXACT = NUM_BUCKETS // 2

LANES = 128
SUBLANES = 8
VMEM_LIMIT_BYTES = 56 * 1024 * 1024

Q_BLOCK = 128
K_TILE = 256
N_BISECT = 14
NEG_INF = float("-inf")
POS_INF = float("inf")
M_INIT = -1e30


def _rms(x, g):
    ms = jnp.mean(x * x, axis=-1, keepdims=True)
    return x * lax.rsqrt(ms + EPS) * g


def _dot(a, b):
    return jnp.dot(a, b, preferred_element_type=F32)


def _dot_nt(a, b):
    return lax.dot_general(a, b, (((1,), (1,)), ((), ())), preferred_element_type=F32)


def _const_spec(shape):
    nd = len(shape)
    return pl.BlockSpec(shape, lambda *_: (0,) * nd, pipeline_mode=pl.Buffered(1))


def _layer_spec(shape, layer):
    nd = len(shape)
    return pl.BlockSpec((None, *shape), lambda *_: (layer,) + (0,) * nd,
                        pipeline_mode=pl.Buffered(1))


def _mlp_tile(x, y, wpre_ref, g_ref, wup_ref, wdn_ref, gfin_ref, *, ff_chunk, final):
    x1 = x + _dot(y, wpre_ref[...])
    hn = _rms(x1, g_ref[...]).astype(BF16)
    acc = x1
    d_ff = wup_ref.shape[1]
    for c in range(d_ff // ff_chunk):
        h = _dot(hn, wup_ref[:, c * ff_chunk:(c + 1) * ff_chunk])
        h = jnp.maximum(h, 0.0)
        h = (h * h).astype(BF16)
        acc = acc + _dot(h, wdn_ref[c * ff_chunk:(c + 1) * ff_chunk, :])
    if final:
        acc = _rms(acc, gfin_ref[...])
    return acc


def _mlp_body(x_ref, y_ref, wpre_ref, g_ref, wup_ref, wdn_ref, gfin_ref, o_ref, *, ff_chunk, final):
    o_ref[...] = _mlp_tile(x_ref[...], y_ref[...], wpre_ref, g_ref, wup_ref, wdn_ref, gfin_ref,
                           ff_chunk=ff_chunk, final=final)


def _mlp_layer(x2d, y2d, w_pre, g, w_up, w_dn, g_fin, *, layer, final, tm=1024, ff_chunk=512):
    n, d = x2d.shape
    d_ff = w_up.shape[-1]
    tm = min(tm, n)
    return pl.pallas_call(
        functools.partial(_mlp_body, ff_chunk=min(ff_chunk, d_ff), final=final),
        grid=(n // tm,),
        in_specs=[
            pl.BlockSpec((tm, d), lambda i: (i, 0)),
            pl.BlockSpec((tm, d), lambda i: (i, 0)),
            _const_spec((d, d)),
            _const_spec((1, d)),
            _layer_spec((d, d_ff), layer),
            _layer_spec((d_ff, d), layer),
            _const_spec((1, d)),
        ],
        out_specs=pl.BlockSpec((tm, d), lambda i: (i, 0)),
        out_shape=jax.ShapeDtypeStruct((n, d), F32),
        compiler_params=pltpu.CompilerParams(
            dimension_semantics=("arbitrary",), vmem_limit_bytes=VMEM_LIMIT_BYTES),
        name="mlp_block",
    )(x2d, y2d, w_pre, g, w_up, w_dn, g_fin)


def _gelu_tanh(x):
    c = math.sqrt(2.0 / math.pi)
    return 0.5 * x * (1.0 + jnp.tanh(c * (x + 0.044715 * (x * x * x))))


def _sigmoid(x):
    return 1.0 / (1.0 + jnp.exp(-x))


def _rec_block(n, yg, xr, cw_ref, cb_ref, wa_ref, ba_ref, wx_ref, bx_ref, lam_ref,
               xbuf, hcar, y_out):
    ts, bw = yg.shape
    cs = slice(n * bw, (n + 1) * bw)
    gate = _gelu_tanh(yg)

    xbuf[SUBLANES:SUBLANES + ts, cs] = xr
    cw = cw_ref[:, cs]
    xc = cb_ref[:, cs] + xbuf[pl.ds(SUBLANES - 3, ts), cs] * cw[0:1]
    xc = xc + xbuf[pl.ds(SUBLANES - 2, ts), cs] * cw[1:2]
    xc = xc + xbuf[pl.ds(SUBLANES - 1, ts), cs] * cw[2:3]
    xc = xc + xr * cw[3:4]
    xbuf[0:SUBLANES, cs] = xbuf[ts:ts + SUBLANES, cs]

    xcb = xc.astype(BF16)
    r = _sigmoid(_dot(xcb, wa_ref[n]) + ba_ref[:, cs])
    ig = _sigmoid(_dot(xcb, wx_ref[n]) + bx_ref[:, cs])
    nl = -lam_ref[:, cs]
    softplus = jnp.maximum(nl, 0.0) + jnp.log1p(jnp.exp(-jnp.abs(nl)))
    log_a = (-RG_C * r) * softplus
    a = jnp.exp(log_a)
    v = -jnp.tanh(log_a) * (a * a + 1.0)
    root = jnp.where(v > 0.0, v * lax.rsqrt(v), 0.0)
    u = root * (ig * xc)

    row = lax.broadcasted_iota(jnp.int32, (SUBLANES, bw), 0)
    masks = [(s, row >= s) for s in (1, 2, 4)]
    hp = hcar[:, cs]
    hs = []
    for c in range(ts // SUBLANES):
        av = a[c * SUBLANES:(c + 1) * SUBLANES]
        uv = u[c * SUBLANES:(c + 1) * SUBLANES]
        for s, m in masks:
            a_sh = jnp.where(m, pltpu.roll(av, s, 0), 1.0)
            u_sh = jnp.where(m, pltpu.roll(uv, s, 0), 0.0)
            uv = av * u_sh + uv
            av = av * a_sh
        hv = av * hp + uv
        hs.append(hv)
        hp = hv[SUBLANES - 1:SUBLANES, :]
    hcar[:, cs] = hp
    y_out[:, cs] = (jnp.concatenate(hs, axis=0) * gate).astype(BF16)


def _rec_mlp_body(x_ref, xp_ref, g_ref, win_ref, cw_ref, cb_ref, wa_ref, ba_ref, wx_ref, bx_ref,
                  lam_ref, wpre_ref, gm_ref, wup_ref, wdn_ref, gfin_ref, o_ref,
                  xbuf, hcar, y_s, *, ts, tiles_per_seq, ff_chunk, final):
    t = pl.program_id(0)
    d = x_ref.shape[-1]
    d_ff = wup_ref.shape[1]
    n_chunks = d_ff // ff_chunk

    @pl.when(lax.rem(t, tiles_per_seq) == 0)
    def _():
        xbuf[0:SUBLANES, :] = jnp.zeros((SUBLANES, d), F32)
        hcar[...] = jnp.zeros((1, d), F32)

    @pl.when(t == 0)
    def _():
        y_s[...] = jnp.zeros(y_s.shape, BF16)

    y2 = _dot(_rms(x_ref[...], g_ref[...]).astype(BF16), win_ref[...])
    bw = d // RG_BLOCKS
    x1 = xp_ref[...] + _dot(y_s[...], wpre_ref[...])
    hn = _rms(x1, gm_ref[...]).astype(BF16)
    acc = x1
    hs = {}

    def up(c):
        h = jnp.maximum(_dot(hn, wup_ref[:, c * ff_chunk:(c + 1) * ff_chunk]), 0.0)
        hs[c] = (h * h).astype(BF16)

    def down(c):
        return _dot(hs.pop(c), wdn_ref[c * ff_chunk:(c + 1) * ff_chunk, :])

    mlp_ops = []
    for c in range(n_chunks):
        if c + 1 < n_chunks:
            mlp_ops.append((up, c + 1))
        mlp_ops.append((down, c))
    up(0)
    for n in range(RG_BLOCKS):
        _rec_block(n, y2[:, n * bw:(n + 1) * bw], y2[:, d + n * bw:d + (n + 1) * bw],
                   cw_ref, cb_ref, wa_ref, ba_ref, wx_ref, bx_ref, lam_ref, xbuf, hcar, y_s)
        take = -(-len(mlp_ops) // (RG_BLOCKS - n))
        for fn, c in mlp_ops[:take]:
            r = fn(c)
            if r is not None:
                acc = acc + r
        mlp_ops = mlp_ops[take:]
    if final:
        acc = _rms(acc, gfin_ref[...])
    o_ref[...] = acc


def _rec_mlp_layer(x, g, w_in, conv_w, conv_b, w_a, b_a, w_x, b_x, lam,
                   w_pre, g_mlp, w_up, w_dn, g_fin, *, layer, final, ts=256, ff_chunk=1024):
    b, s, d = x.shape
    ts = min(ts, s)
    bw = d // RG_BLOCKS
    d_ff = w_up.shape[-1]
    n_tiles = b * s // ts
    x2d = x.reshape(b * s, d)
    cur = lambda t: (jnp.minimum(t, n_tiles - 1), 0)
    prev = lambda t: (jnp.maximum(t - 1, 0), 0)
    out = pl.pallas_call(
        functools.partial(_rec_mlp_body, ts=ts, tiles_per_seq=s // ts,
                          ff_chunk=min(ff_chunk, d_ff), final=final),
        grid=(n_tiles + 1,),
        in_specs=[
            pl.BlockSpec((ts, d), cur),
            pl.BlockSpec((ts, d), prev),
            _const_spec((1, d)),
            _const_spec((d, 2 * d)),
            _const_spec((CONV_W, d)),
            _const_spec((1, d)),
            _const_spec((RG_BLOCKS, bw, bw)),
            _const_spec((1, d)),
            _const_spec((RG_BLOCKS, bw, bw)),
            _const_spec((1, d)),
            _const_spec((1, d)),
            _const_spec((d, d)),
            _const_spec((1, d)),
            _layer_spec((d, d_ff), layer),
            _layer_spec((d_ff, d), layer),
            _const_spec((1, d)),
        ],
        out_specs=pl.BlockSpec((ts, d), prev),
        out_shape=jax.ShapeDtypeStruct((b * s, d), F32),
        scratch_shapes=[
            pltpu.VMEM((ts + SUBLANES, d), F32),
            pltpu.VMEM((1, d), F32),
            pltpu.VMEM((ts, d), BF16),
        ],
        compiler_params=pltpu.CompilerParams(
            dimension_semantics=("arbitrary",), vmem_limit_bytes=VMEM_LIMIT_BYTES,
        ),
        name="rglru_mlp",
    )(x2d, x2d, g, w_in, conv_w, conv_b, w_a, b_a, w_x, b_x, lam, w_pre, g_mlp, w_up, w_dn, g_fin)
    return out.reshape(b, s, d)


def _attproj_body(x_ref, g_ref, w_ref, gkv_ref, wuk_ref,
                  qlat_ref, caug_ref, qi_ref, kia_ref, kib_ref, wi_ref):
    dq = H_ATT * D_QK
    di = H_IDX * D_IDX
    hn = _rms(x_ref[0], g_ref[...]).astype(BF16)
    y = _dot(hn, w_ref[...])
    ts = y.shape[0]
    o = dq
    craw = y[:, o:o + D_LAT]
    o += D_LAT
    qi_ref[0] = y[:, o:o + di].astype(BF16)
    o += di
    tail = y[:, o:o + LANES]
    moved = pltpu.roll(tail, D_IDX, 1)
    lane = lax.broadcasted_iota(jnp.int32, tail.shape, 1)
    kia_ref[0] = jnp.where(lane < D_IDX, tail, 0.0).astype(BF16)
    kib_ref[0] = jnp.where(lane >= D_IDX, moved, 0.0).astype(BF16)
    wi_ref[0] = jnp.where(lane < H_IDX, moved, 0.0) * IDX_W_SCALE

    c = _rms(craw, gkv_ref[...])
    caug_ref[0] = jnp.concatenate([c, jnp.ones((ts, D_LAT), F32)], axis=1).astype(BF16)

    for m in range(H_ATT // 2):
        qp = y[:, m * LANES:(m + 1) * LANES].astype(BF16)
        ql = _dot(qp, wuk_ref[m]) * (ATT_SCALE * LOG2E)
        qlat_ref[0, 2 * m] = ql[:, :D_LAT].astype(BF16)
        qlat_ref[0, 2 * m + 1] = ql[:, D_LAT:].astype(BF16)


def _att_proj(x, g, w_pack, g_kv, wuk2, *, ts=512):
    b, s, d = x.shape
    ts = min(ts, s)
    n_out = w_pack.shape[1]
    di = H_IDX * D_IDX
    row = lambda i, j: (i, j, 0)
    return pl.pallas_call(
        _attproj_body,
        grid=(b, s // ts),
        in_specs=[
            pl.BlockSpec((1, ts, d), row),
            _const_spec((1, d)),
            _const_spec((d, n_out)),
            _const_spec((1, D_LAT)),
            _const_spec((H_ATT // 2, 2 * D_QK, 2 * D_LAT)),
        ],
        out_specs=[
            pl.BlockSpec((1, H_ATT, ts, D_LAT), lambda i, j: (i, 0, j, 0)),
            pl.BlockSpec((1, ts, 2 * D_LAT), row),
            pl.BlockSpec((1, ts, di), row),
            pl.BlockSpec((1, ts, LANES), row),
            pl.BlockSpec((1, ts, LANES), row),
            pl.BlockSpec((1, ts, LANES), row),
        ],
        out_shape=[
            jax.ShapeDtypeStruct((b, H_ATT, s, D_LAT), BF16),
            jax.ShapeDtypeStruct((b, s, 2 * D_LAT), BF16),
            jax.ShapeDtypeStruct((b, s, di), BF16),
            jax.ShapeDtypeStruct((b, s, LANES), BF16),
            jax.ShapeDtypeStruct((b, s, LANES), BF16),
            jax.ShapeDtypeStruct((b, s, LANES), F32),
        ],
        compiler_params=pltpu.CompilerParams(
            dimension_semantics=("arbitrary", "arbitrary"), vmem_limit_bytes=VMEM_LIMIT_BYTES),
        name="att_proj",
    )(x, g, w_pack, g_kv, wuk2)


def _t5_bucket(dist):
    n = jnp.maximum(dist, 0)
    nf = jnp.maximum(n, 1).astype(F32)
    large = MAX_EXACT + jnp.floor(jnp.log(nf / MAX_EXACT) / math.log(MAX_DISTANCE / MAX_EXACT)
                                  * (NUM_BUCKETS - MAX_EXACT)).astype(jnp.int32)
    large = jnp.minimum(large, NUM_BUCKETS - 1)
    return jnp.where(n < MAX_EXACT, n, large)


def _attn_body(rb_ref, qlat_ref, qi_ref, wi_ref, caug_ref, kia_ref, kib_ref, wuv_ref, o_ref,
               sc_std, sc_t, acc, mrun, alph, btab, wib, p_s, tj, thr, jthr, *, topk, seq):
    tq, tk = Q_BLOCK, K_TILE
    nb = tk // tq
    step = pl.program_id(1)
    n_kt = step + 1
    hq = H_ATT * tq
    pairs = H_IDX // 2

    @pl.when((pl.program_id(0) == 0) & (step == 0))
    def _():
        ql = lax.broadcasted_iota(jnp.int32, (tq, tq), 0)
        kl = lax.broadcasted_iota(jnp.int32, (tq, tq), 1)
        for var in range(2):
            bucket = _t5_bucket(ql - kl + var * tq)
            for h in range(H_ATT):
                t = jnp.zeros((tq, tq), F32)
                for bk in range(NUM_BUCKETS - 1):
                    t = jnp.where(bucket == bk, rb_ref[bk, h] - rb_ref[NUM_BUCKETS - 1, h], t)
                btab[h, var] = t * LOG2E

    qi = qi_ref[0]
    qi8 = jnp.concatenate(
        [qi[blk * tq:(blk + 1) * tq, m * LANES:(m + 1) * LANES]
         for blk in range(nb) for m in range(pairs)], axis=0)
    wi = wi_ref[0]
    for blk in range(nb):
        for h in range(H_IDX):
            wib[blk, h] = jnp.broadcast_to(wi[blk * tq:(blk + 1) * tq, h:h + 1], (tq, tk))
    row_q = lax.broadcasted_iota(jnp.int32, (tq, tk), 0)
    lane_k = lax.broadcasted_iota(jnp.int32, (tq, tk), 1)

    def score_tile(kt, carry, *, last):
        k0 = pl.multiple_of(kt * tk, tk)
        da = jnp.maximum(_dot_nt(qi8, kia_ref[0, pl.ds(k0, tk), :]), 0.0)
        db = jnp.maximum(_dot_nt(qi8, kib_ref[0, pl.ds(k0, tk), :]), 0.0)
        for blk in range(nb):
            s = jnp.zeros((tq, tk), F32)
            for m in range(pairs):
                r0 = (blk * pairs + m) * tq
                s = s + wib[blk, 2 * m] * da[r0:r0 + tq]
                s = s + wib[blk, 2 * m + 1] * db[r0:r0 + tq]
            if last:
                s = jnp.where(lane_k <= row_q + blk * tq, s, NEG_INF)
            sc_std[blk, kt] = s
            sc_t[kt, :, blk * tq:(blk + 1) * tq] = s.T
        return carry

    def score_pair(j, carry):
        score_tile(2 * j, carry, last=False)
        return score_tile(2 * j + 1, carry, last=False)

    lax.fori_loop(0, lax.shift_right_logical(step, 1), score_pair, 0)

    @pl.when((step & 1) == 1)
    def _():
        score_tile(step - 1, 0, last=False)

    score_tile(step, 0, last=True)

    st = (SUBLANES, nb * tq)
    p_lane = step * (nb * tq) + lax.broadcasted_iota(jnp.int32, st, 1)
    kq = jnp.minimum(topk, p_lane + 1).astype(F32)

    def search(n):
        def over_tiles(fn, op):
            pair = {jnp.sum: jnp.add, jnp.max: jnp.maximum, jnp.min: jnp.minimum}[op]
            parts = [op(fn(sc_t[kt].reshape(tk // SUBLANES, SUBLANES, nb * tq), kt), axis=0)
                     for kt in range(n)]
            while len(parts) > 1:
                parts = [pair(*parts[i:i + 2]) if i + 1 < len(parts) else parts[i]
                         for i in range(0, len(parts), 2)]
            x = parts[0]
            for s in (4, 2, 1):
                x = pair(x, pltpu.roll(x, s, 0))
            return x

        def count_ge(th):
            return over_tiles(lambda v, kt: jnp.where(v >= th, 1.0, 0.0), jnp.sum)

        mx = over_tiles(lambda v, kt: v, jnp.max)
        lo0 = over_tiles(lambda v, kt: jnp.where(v == NEG_INF, POS_INF, v), jnp.min)

        def bisect(_, c):
            lo, hix, chi = c
            hib = jnp.where(hix == POS_INF, mx, hix)
            mid = 0.5 * lo + 0.5 * hib
            cnt = count_ge(mid)
            ge = cnt >= kq
            return jnp.where(ge, mid, lo), jnp.where(ge, hix, mid), jnp.where(ge, chi, cnt)

        _, hix, chi = lax.fori_loop(
            0, N_BISECT, bisect,
            (lo0, jnp.full(st, POS_INF, F32), jnp.zeros(st, F32)))

        def sweep(cand):
            cs, ms = [], []
            for kt in range(n):
                v = sc_t[kt].reshape(tk // SUBLANES, SUBLANES, nb * tq)
                ge = v >= cand
                cs.append(jnp.sum(jnp.where(ge, 1.0, 0.0), axis=0))
                ms.append(jnp.max(jnp.where(ge, NEG_INF, v), axis=0))
            c, m = functools.reduce(jnp.add, cs), functools.reduce(jnp.maximum, ms)
            for s in (4, 2, 1):
                c = c + pltpu.roll(c, s, 0)
                m = jnp.maximum(m, pltpu.roll(m, s, 0))
            return c, m

        def snap_cond(c):
            return jnp.max(c[5]) > 0.0

        def snap_body(c):
            cand, chi, t, cge, ngt, pending = c
            ct, nxt = sweep(cand)
            act = pending > 0.0
            ok = ct >= kq
            t = jnp.where(act, cand, t)
            cge = jnp.where(act, ct, cge)
            ngt = jnp.where(act, chi, ngt)
            pending = jnp.where(act & ok, 0.0, pending)
            cand = jnp.where(act & ~ok, nxt, cand)
            chi = jnp.where(act & ~ok, ct, chi)
            return cand, chi, t, cge, ngt, pending

        cand0 = over_tiles(lambda v, kt: jnp.where(v < hix, v, NEG_INF), jnp.max)
        zero = jnp.zeros(st, F32)
        first_sweep = snap_body((cand0, chi, zero, zero, zero, jnp.ones(st, F32)))
        _, _, t_k, cge, ngt, _ = lax.while_loop(snap_cond, snap_body, first_sweep)
        thr[...] = t_k
        jthr[...] = jnp.full(st, float(seq), F32)

        need = kq - ngt
        excess = (cge - ngt) > need

        @pl.when(jnp.max(jnp.where(excess, 1.0, 0.0)) > 0.0)
        def _():
            sub_k = lax.broadcasted_iota(jnp.int32, (tk, nb * tq), 0).astype(F32).reshape(
                tk // SUBLANES, SUBLANES, nb * tq)

            def jbisect(_, c):
                jlo, jhi = c
                mid = jnp.floor(0.5 * (jlo + jhi))
                cnt = over_tiles(
                    lambda v, kt: jnp.where((v == t_k) & (sub_k + float(kt * tk) <= mid), 1.0, 0.0),
                    jnp.sum)
                ge = cnt >= need
                return jnp.where(ge, jlo, mid), jnp.where(ge, mid, jhi)

            n_pass = max(1, int(math.ceil(math.log2(seq))))
            _, jhi = lax.fori_loop(
                0, n_pass, jbisect,
                (jnp.full(st, -1.0, F32), jnp.full(st, float(seq - 1), F32)))
            jthr[...] = jhi

    for n in range(1, seq // tk + 1):
        pl.when(n_kt == n)(functools.partial(search, n))

    for blk in range(nb):
        for i, ref in enumerate((thr, jthr)):
            v = jnp.concatenate([ref[:, blk * tq:(blk + 1) * tq]] * (tq // SUBLANES), axis=0).T
            tj[blk, i] = jnp.concatenate([v] * (tk // tq), axis=1)

    lane_kf = lane_k.astype(F32)

    def attend_tile(kt, carry=None, *, kind, first=False):
        qall = jnp.concatenate(
            [qlat_ref[0, :, blk * tq:(blk + 1) * tq, :].reshape(hq, D_LAT) for blk in range(nb)],
            axis=0)
        k0 = pl.multiple_of(kt * tk, tk)
        ct = caug_ref[0, pl.ds(k0, tk), :]
        src = _dot_nt(qall, ct[:, :D_LAT])
        kf = lane_kf + (kt * tk).astype(F32)
        for blk in range(nb):
            s = sc_std[blk, kt]
            tb = tj[blk, 0]
            sel = (s > tb) | ((s == tb) & (kf <= tj[blk, 1]))
            ma = jnp.where(sel, 0.0, NEG_INF)
            for h in range(H_ATT):
                r0 = (blk * H_ATT + h) * tq
                lh = src[r0:r0 + tq, :] + ma
                if kind == "near" and blk == 0:
                    lh = jnp.concatenate([lh[:, :tq], lh[:, tq:] + btab[h, 1]], axis=1)
                elif kind == "last" and blk == 0:
                    lh = jnp.concatenate([lh[:, :tq] + btab[h, 0], lh[:, tq:]], axis=1)
                elif kind == "last" and blk == 1:
                    lh = lh + jnp.concatenate([btab[h, 1], btab[h, 0]], axis=1)
                mt = jnp.maximum(lh[:, :tq], lh[:, tq:])
                rm = jnp.broadcast_to(jnp.max(mt, axis=1, keepdims=True), (tq, tq))
                if first:
                    m_new = jnp.maximum(rm, M_INIT)
                else:
                    m_old = mrun[blk, h]
                    m_new = jnp.maximum(m_old, rm)
                    alph[blk, h] = jnp.exp2(m_old - m_new)
                mrun[blk, h] = m_new
                m2 = jnp.concatenate([m_new, m_new], axis=1)
                p_s[r0:r0 + tq, :] = jnp.exp2(lh - m2).astype(BF16)
        pv = _dot(p_s[...], ct)
        if first:
            acc[...] = pv
            return carry
        for blk in range(nb):
            for h in range(H_ATT):
                r0 = (blk * H_ATT + h) * tq
                al = alph[blk, h]
                acc[r0:r0 + tq, :] = acc[r0:r0 + tq, :] * jnp.concatenate([al, al], axis=1) + pv[r0:r0 + tq]
        return carry

    attend_tile(step, kind="last", first=True)

    @pl.when(step >= 1)
    def _():
        attend_tile(step - 1, kind="near")

    lax.fori_loop(0, jnp.maximum(step - 1, 0), functools.partial(attend_tile, kind="far"), 0)

    for m in range(H_ATT // 2):
        rows = []
        for blk in range(nb):
            halves = []
            for h in (2 * m, 2 * m + 1):
                r0 = (blk * H_ATT + h) * tq
                a = acc[r0:r0 + tq, :]
                halves.append(a[:, :D_LAT] * (1.0 / a[:, D_LAT:]))
            rows.append(jnp.concatenate(halves, axis=1))
        pair = jnp.concatenate(rows, axis=0).astype(BF16)
        o_ref[0, :, m * 2 * D_V:(m + 1) * 2 * D_V] = _dot(pair, wuv_ref[m]).astype(BF16)


def _attention(rel_bias, qlat, qi, wi, caug, kia, kib, wuv2, *, topk):
    b, _, s, _ = qlat.shape
    tq, tk = Q_BLOCK, K_TILE
    assert s % tk == 0 and tk == 2 * tq
    nb = tk // tq
    n_kt = s // tk
    hq = H_ATT * tq
    di = H_IDX * D_IDX
    dv = H_ATT * D_V
    qrow = lambda i, j: (i, j, 0)
    full = lambda i, j: (i, 0, 0)
    return pl.pallas_call(
        functools.partial(_attn_body, topk=topk, seq=s),
        grid=(b, s // tk),
        in_specs=[
            pl.BlockSpec(memory_space=pltpu.SMEM),
            pl.BlockSpec((1, H_ATT, tk, D_LAT), lambda i, j: (i, 0, j, 0)),
            pl.BlockSpec((1, tk, di), qrow),
            pl.BlockSpec((1, tk, LANES), qrow),
            pl.BlockSpec((1, s, 2 * D_LAT), full),
            pl.BlockSpec((1, s, LANES), full),
            pl.BlockSpec((1, s, LANES), full),
            _const_spec((H_ATT // 2, 2 * D_LAT, 2 * D_V)),
        ],
        out_specs=pl.BlockSpec((1, tk, dv), qrow),
        out_shape=jax.ShapeDtypeStruct((b, s, dv), BF16),
        scratch_shapes=[
            pltpu.VMEM((nb, n_kt, tq, tk), F32),
            pltpu.VMEM((n_kt, tk, nb * tq), F32),
            pltpu.VMEM((nb * hq, tk), F32),
            pltpu.VMEM((nb, H_ATT, tq, tq), F32),
            pltpu.VMEM((nb, H_ATT, tq, tq), F32),
            pltpu.VMEM((H_ATT, 2, tq, tq), F32),
            pltpu.VMEM((nb, H_IDX, tq, tk), F32),
            pltpu.VMEM((nb * hq, tk), BF16),
            pltpu.VMEM((nb, 2, tq, tk), F32),
            pltpu.VMEM((SUBLANES, nb * tq), F32),
            pltpu.VMEM((SUBLANES, nb * tq), F32),
        ],
        compiler_params=pltpu.CompilerParams(
            dimension_semantics=("arbitrary", "arbitrary"), vmem_limit_bytes=VMEM_LIMIT_BYTES),
        name="dsa_attention",
    )(rel_bias, qlat, qi, wi, caug, kia, kib, wuv2)


def _block_diag_pairs(w):
    h, a, b = w.shape
    w = w.reshape(h // 2, 2, a, b)
    z = jnp.zeros((h // 2, a, b), w.dtype)
    top = jnp.concatenate([w[:, 0], z], axis=2)
    bot = jnp.concatenate([z, w[:, 1]], axis=2)
    return jnp.concatenate([top, bot], axis=1)


def _pack_att_w(w_in):
    assert 2 * D_IDX == LANES and D_IDX + H_IDX <= LANES
    n = w_in.shape[1]
    return jnp.pad(w_in, ((0, 0), (0, -n % LANES)))


def kernel(x, norm_mix_g, norm_mlp_g, final_norm_g, rec_w_in, rec_conv_w, rec_conv_b, rec_w_a, rec_b_a, rec_w_x, rec_b_x, rec_lambda, rec_w_out, att_w_in, att_kv_norm_g, att_w_uk, att_w_uv, att_w_o, rel_bias, mlp_w_up, mlp_w_down):
    b, s, d = x.shape
    depth = norm_mix_g.shape[0]
    topk = min(TOPK_MAX, s // 4)
    row = lambda v: v.reshape(1, -1).astype(F32)
    w_up_all = mlp_w_up.astype(BF16)
    w_dn_all = mlp_w_down.astype(BF16)
    for layer in range(depth):
        j = layer // 2
        g_mix = row(norm_mix_g[layer])
        final = layer == depth - 1
        mlp_w = (row(norm_mlp_g[layer]), w_up_all, w_dn_all, row(final_norm_g))
        if layer % 2 == 0:
            x = _rec_mlp_layer(
                x, g_mix, rec_w_in[j].astype(BF16), rec_conv_w[j].astype(F32), row(rec_conv_b[j]),
                rec_w_a[j].astype(BF16), row(rec_b_a[j]), rec_w_x[j].astype(BF16), row(rec_b_x[j]),
                row(rec_lambda[j]), rec_w_out[j].astype(BF16), *mlp_w, layer=layer, final=final)
        else:
            qlat, caug, qi, kia, kib, wi = _att_proj(
                x, g_mix, _pack_att_w(att_w_in[j].astype(BF16)), row(att_kv_norm_g[j]),
                _block_diag_pairs(att_w_uk[j]).astype(BF16))
            y = _attention(
                rel_bias.astype(F32), qlat, qi, wi, caug, kia, kib,
                _block_diag_pairs(att_w_uv[j]).astype(BF16), topk=topk)
            x = _mlp_layer(
                x.reshape(b * s, d), y.reshape(b * s, d), att_w_o[j].astype(BF16), *mlp_w,
                layer=layer, final=final).reshape(b, s, d)
    return x
```

```python
import functools
import math

import jax
import jax.numpy as jnp
from jax import lax
from jax.experimental import pallas as pl
from jax.experimental.pallas import tpu as pltpu

F32 = jnp.float32
BF16 = jnp.bfloat16

EPS = 1e-6
RG_BLOCKS = 4
CONV_W = 4
RG_C = 8.0
H_ATT = 16
D_QK = 64
D_V = 64
D_LAT = 128
H_IDX = 8
D_IDX = 64
TOPK_MAX = 256
ATT_SCALE = D_QK ** -0.5
LOG2E = math.log2(math.e)
IDX_W_SCALE = (H_IDX ** -0.5) * (D_IDX ** -0.5)
NUM_BUCKETS = 32
MAX_DISTANCE = 128
MAX_EXACT = NUM_BUCKETS // 2

LANES = 128
SUBLANES = 8
VMEM_LIMIT_BYTES = 56 * 1024 * 1024

Q_BLOCK = 128
K_TILE = 256
N_BISECT = 14
NEG_INF = float("-inf")
POS_INF = float("inf")
M_INIT = -1e30


def _rms(x, g):
    ms = jnp.mean(x * x, axis=-1, keepdims=True)
    return x * lax.rsqrt(ms + EPS) * g


def _dot(a, b):
    return jnp.dot(a, b, preferred_element_type=F32)


def _dot_nt(a, b):
    return lax.dot_general(a, b, (((1,), (1,)), ((), ())), preferred_element_type=F32)


def _const_spec(shape):
    nd = len(shape)
    return pl.BlockSpec(shape, lambda *_: (0,) * nd, pipeline_mode=pl.Buffered(1))


def _layer_spec(shape, layer):
    nd = len(shape)
    return pl.BlockSpec((None, *shape), lambda *_: (layer,) + (0,) * nd,
                        pipeline_mode=pl.Buffered(1))


def _mlp_tile(x, y, wpre_ref, g_ref, wup_ref, wdn_ref, gfin_ref, *, ff_chunk, final):
    x1 = x + _dot(y, wpre_ref[...])
    hn = _rms(x1, g_ref[...]).astype(BF16)
    acc = x1
    d_ff = wup_ref.shape[1]
    for c in range(d_ff // ff_chunk):
        h = _dot(hn, wup_ref[:, c * ff_chunk:(c + 1) * ff_chunk])
        h = jnp.maximum(h, 0.0)
        h = (h * h).astype(BF16)
        acc = acc + _dot(h, wdn_ref[c * ff_chunk:(c + 1) * ff_chunk, :])
    if final:
        acc = _rms(acc, gfin_ref[...])
    return acc


def _mlp_body(x_ref, y_ref, wpre_ref, g_ref, wup_ref, wdn_ref, gfin_ref, o_ref, *, ff_chunk, final):
    o_ref[...] = _mlp_tile(x_ref[...], y_ref[...], wpre_ref, g_ref, wup_ref, wdn_ref, gfin_ref,
                           ff_chunk=ff_chunk, final=final)


def _mlp_layer(x2d, y2d, w_pre, g, w_up, w_dn, g_fin, *, layer, final, tm=512, ff_chunk=512):
    n, d = x2d.shape
    d_ff = w_up.shape[-1]
    tm = min(tm, n)
    return pl.pallas_call(
        functools.partial(_mlp_body, ff_chunk=min(ff_chunk, d_ff), final=final),
        grid=(n // tm,),
        in_specs=[
            pl.BlockSpec((tm, d), lambda i: (i, 0)),
            pl.BlockSpec((tm, d), lambda i: (i, 0)),
            _const_spec((d, d)),
            _const_spec((1, d)),
            _layer_spec((d, d_ff), layer),
            _layer_spec((d_ff, d), layer),
            _const_spec((1, d)),
        ],
        out_specs=pl.BlockSpec((tm, d), lambda i: (i, 0)),
        out_shape=jax.ShapeDtypeStruct((n, d), F32),
        compiler_params=pltpu.CompilerParams(
            dimension_semantics=("arbitrary",), vmem_limit_bytes=VMEM_LIMIT_BYTES),
        name="mlp_block",
    )(x2d, y2d, w_pre, g, w_up, w_dn, g_fin)


def _gelu_tanh(x):
    c = math.sqrt(2.0 / math.pi)
    return 0.5 * x * (1.0 + jnp.tanh(c * (x + 0.044715 * (x * x * x))))


def _sigmoid(x):
    return 1.0 / (1.0 + jnp.exp(-x))


def _rec_block(n, yg, xr, cw_ref, cb_ref, wa_ref, ba_ref, wx_ref, bx_ref, lam_ref,
               xbuf, hcar, y_out):
    ts, bw = yg.shape
    cs = slice(n * bw, (n + 1) * bw)
    gate = _gelu_tanh(yg)

    xbuf[SUBLANES:SUBLANES + ts, cs] = xr
    cw = cw_ref[:, cs]
    xc = cb_ref[:, cs] + xbuf[pl.ds(SUBLANES - 3, ts), cs] * cw[0:1]
    xc = xc + xbuf[pl.ds(SUBLANES - 2, ts), cs] * cw[1:2]
    xc = xc + xbuf[pl.ds(SUBLANES - 1, ts), cs] * cw[2:3]
    xc = xc + xr * cw[3:4]
    xbuf[0:SUBLANES, cs] = xbuf[ts:ts + SUBLANES, cs]

    xcb = xc.astype(BF16)
    r = _sigmoid(_dot(xcb, wa_ref[n]) + ba_ref[:, cs])
    ig = _sigmoid(_dot(xcb, wx_ref[n]) + bx_ref[:, cs])
    nl = -lam_ref[:, cs]
    softplus = jnp.maximum(nl, 0.0) + jnp.log1p(jnp.exp(-jnp.abs(nl)))
    log_a = (-RG_C * r) * softplus
    a = jnp.exp(log_a)
    v = -jnp.tanh(log_a) * (a * a + 1.0)
    root = jnp.where(v > 0.0, v * lax.rsqrt(v), 0.0)
    u = root * (ig * xc)

    row = lax.broadcasted_iota(jnp.int32, (SUBLANES, bw), 0)
    masks = [(s, row >= s) for s in (1, 2, 4)]
    hp = hcar[:, cs]
    hs = []
    for c in range(ts // SUBLANES):
        av = a[c * SUBLANES:(c + 1) * SUBLANES]
        uv = u[c * SUBLANES:(c + 1) * SUBLANES]
        for s, m in masks:
            a_sh = jnp.where(m, pltpu.roll(av, s, 0), 1.0)
            u_sh = jnp.where(m, pltpu.roll(uv, s, 0), 0.0)
            uv = av * u_sh + uv
            av = av * a_sh
        hv = av * hp + uv
        hs.append(hv)
        hp = hv[SUBLANES - 1:SUBLANES, :]
    hcar[:, cs] = hp
    y_out[:, cs] = (jnp.concatenate(hs, axis=0) * gate).astype(BF16)


def _rec_mlp_body(x_ref, xp_ref, g_ref, win_ref, cw_ref, cb_ref, wa_ref, ba_ref, wx_ref, bx_ref,
                  lam_ref, wpre_ref, gm_ref, wup_ref, wdn_ref, gfin_ref, o_ref,
                  xbuf, hcar, y_s, *, ts, tiles_per_seq, ff_chunk, final):
    t = pl.program_id(0)
    d = x_ref.shape[-1]
    d_ff = wup_ref.shape[1]
    n_chunks = d_ff // ff_chunk

    @pl.when(lax.rem(t, tiles_per_seq) == 0)
    def _():
        xbuf[0:SUBLANES, :] = jnp.zeros((SUBLANES, d), F32)
        hcar[...] = jnp.zeros((1, d), F32)

    @pl.when(t == 0)
    def _():
        y_s[...] = jnp.zeros(y_s.shape, BF16)

    y2 = _dot(_rms(x_ref[...], g_ref[...]).astype(BF16), win_ref[...])
    bw = d // RG_BLOCKS
    x1 = xp_ref[...] + _dot(y_s[...], wpre_ref[...])
    hn = _rms(x1, gm_ref[...]).astype(BF16)
    acc = x1
    hs = {}

    def up(c):
        h = jnp.maximum(_dot(hn, wup_ref[:, c * ff_chunk:(c + 1) * ff_chunk]), 0.0)
        hs[c] = (h * h).astype(BF16)

    def down(c):
        return _dot(hs.pop(c), wdn_ref[c * ff_chunk:(c + 1) * ff_chunk, :])

    mlp_ops = []
    for c in range(n_chunks):
        if c + 1 < n_chunks:
            mlp_ops.append((up, c + 1))
        mlp_ops.append((down, c))
    up(0)
    for n in range(RG_BLOCKS):
        _rec_block(n, y2[:, n * bw:(n + 1) * bw], y2[:, d + n * bw:d + (n + 1) * bw],
                   cw_ref, cb_ref, wa_ref, ba_ref, wx_ref, bx_ref, lam_ref, xbuf, hcar, y_s)
        take = -(-len(mlp_ops) // (RG_BLOCKS - n))
        for fn, c in mlp_ops[:take]:
            r = fn(c)
            if r is not None:
                acc = acc + r
        mlp_ops = mlp_ops[take:]
    if final:
        acc = _rms(acc, gfin_ref[...])
    o_ref[...] = acc


def _rec_mlp_layer(x, g, w_in, conv_w, conv_b, w_a, b_a, w_x, b_x, lam,
                   w_pre, g_mlp, w_up, w_dn, g_fin, *, layer, final, ts=256, ff_chunk=1024):
    b, s, d = x.shape
    ts = min(ts, s)
    bw = d // RG_BLOCKS
    d_ff = w_up.shape[-1]
    n_tiles = b * s // ts
    x2d = x.reshape(b * s, d)
    cur = lambda t: (jnp.minimum(t, n_tiles - 1), 0)
    prev = lambda t: (jnp.maximum(t - 1, 0), 0)
    out = pl.pallas_call(
        functools.partial(_rec_mlp_body, ts=ts, tiles_per_seq=s // ts,
                          ff_chunk=min(ff_chunk, d_ff), final=final),
        grid=(n_tiles + 1,),
        in_specs=[
            pl.BlockSpec((ts, d), cur),
            pl.BlockSpec((ts, d), prev),
            _const_spec((1, d)),
            _const_spec((d, 2 * d)),
            _const_spec((CONV_W, d)),
            _const_spec((1, d)),
            _const_spec((RG_BLOCKS, bw, bw)),
            _const_spec((1, d)),
            _const_spec((RG_BLOCKS, bw, bw)),
            _const_spec((1, d)),
            _const_spec((1, d)),
            _const_spec((d, d)),
            _const_spec((1, d)),
            _layer_spec((d, d_ff), layer),
            _layer_spec((d_ff, d), layer),
            _const_spec((1, d)),
        ],
        out_specs=pl.BlockSpec((ts, d), prev),
        out_shape=jax.ShapeDtypeStruct((b * s, d), F32),
        scratch_shapes=[
            pltpu.VMEM((ts + SUBLANES, d), F32),
            pltpu.VMEM((1, d), F32),
            pltpu.VMEM((ts, d), BF16),
        ],
        compiler_params=pltpu.CompilerParams(
            dimension_semantics=("arbitrary",), vmem_limit_bytes=VMEM_LIMIT_BYTES,
        ),
        name="rglru_mlp",
    )(x2d, x2d, g, w_in, conv_w, conv_b, w_a, b_a, w_x, b_x, lam, w_pre, g_mlp, w_up, w_dn, g_fin)
    return out.reshape(b, s, d)


def _attproj_body(x_ref, g_ref, w_ref, gkv_ref, wuk_ref,
                  qlat_ref, caug_ref, qi_ref, kia_ref, kib_ref, wi_ref):
    dq = H_ATT * D_QK
    di = H_IDX * D_IDX
    hn = _rms(x_ref[0], g_ref[...]).astype(BF16)
    y = _dot(hn, w_ref[...])
    ts = y.shape[0]
    o = dq
    craw = y[:, o:o + D_LAT]
    o += D_LAT
    qi_ref[0] = y[:, o:o + di].astype(BF16)
    o += di
    tail = y[:, o:o + LANES]
    moved = pltpu.roll(tail, D_IDX, 1)
    lane = lax.broadcasted_iota(jnp.int32, tail.shape, 1)
    kia_ref[0] = jnp.where(lane < D_IDX, tail, 0.0).astype(BF16)
    kib_ref[0] = jnp.where(lane >= D_IDX, moved, 0.0).astype(BF16)
    wi_ref[0] = jnp.where(lane < H_IDX, moved, 0.0) * IDX_W_SCALE

    c = _rms(craw, gkv_ref[...])
    caug_ref[0] = jnp.concatenate([c, jnp.ones((ts, D_LAT), F32)], axis=1).astype(BF16)

    for m in range(H_ATT // 2):
        qp = y[:, m * LANES:(m + 1) * LANES].astype(BF16)
        ql = _dot(qp, wuk_ref[m]) * (ATT_SCALE * LOG2E)
        qlat_ref[0, 2 * m] = ql[:, :D_LAT].astype(BF16)
        qlat_ref[0, 2 * m + 1] = ql[:, D_LAT:].astype(BF16)


def _att_proj(x, g, w_pack, g_kv, wuk2, *, ts=512):
    b, s, d = x.shape
    ts = min(ts, s)
    n_out = w_pack.shape[1]
    di = H_IDX * D_IDX
    row = lambda i, j: (i, j, 0)
    return pl.pallas_call(
        _attproj_body,
        grid=(b, s // ts),
        in_specs=[
            pl.BlockSpec((1, ts, d), row),
            _const_spec((1, d)),
            _const_spec((d, n_out)),
            _const_spec((1, D_LAT)),
            _const_spec((H_ATT // 2, 2 * D_QK, 2 * D_LAT)),
        ],
        out_specs=[
            pl.BlockSpec((1, H_ATT, ts, D_LAT), lambda i, j: (i, 0, j, 0)),
            pl.BlockSpec((1, ts, 2 * D_LAT), row),
            pl.BlockSpec((1, ts, di), row),
            pl.BlockSpec((1, ts, LANES), row),
            pl.BlockSpec((1, ts, LANES), row),
            pl.BlockSpec((1, ts, LANES), row),
        ],
        out_shape=[
            jax.ShapeDtypeStruct((b, H_ATT, s, D_LAT), BF16),
            jax.ShapeDtypeStruct((b, s, 2 * D_LAT), BF16),
            jax.ShapeDtypeStruct((b, s, di), BF16),
            jax.ShapeDtypeStruct((b, s, LANES), BF16),
            jax.ShapeDtypeStruct((b, s, LANES), BF16),
            jax.ShapeDtypeStruct((b, s, LANES), F32),
        ],
        compiler_params=pltpu.CompilerParams(
            dimension_semantics=("arbitrary", "arbitrary"), vmem_limit_bytes=VMEM_LIMIT_BYTES),
        name="att_proj",
    )(x, g, w_pack, g_kv, wuk2)


def _t5_bucket(dist):
    n = jnp.maximum(dist, 0)
    nf = jnp.maximum(n, 1).astype(F32)
    large = MAX_EXACT + jnp.floor(jnp.log(nf / MAX_EXACT) / math.log(MAX_DISTANCE / MAX_EXACT)
                                  * (NUM_BUCKETS - MAX_EXACT)).astype(jnp.int32)
    large = jnp.minimum(large, NUM_BUCKETS - 1)
    return jnp.where(n < MAX_EXACT, n, large)


def _attn_body(rb_ref, qlat_ref, qi_ref, wi_ref, caug_ref, kia_ref, kib_ref, wuv_ref, o_ref,
               sc_std, sc_t, acc, mrun, alph, btab, wib, p_s, tj, thr, jthr, *, topk, seq):
    tq, tk = Q_BLOCK, K_TILE
    nb = tk // tq
    step = pl.program_id(1)
    n_kt = step + 1
    hq = H_ATT * tq
    pairs = H_IDX // 2

    @pl.when((pl.program_id(0) == 0) & (step == 0))
    def _():
        ql = lax.broadcasted_iota(jnp.int32, (tq, tq), 0)
        kl = lax.broadcasted_iota(jnp.int32, (tq, tq), 1)
        for var in range(2):
            bucket = _t5_bucket(ql - kl + var * tq)
            for h in range(H_ATT):
                t = jnp.zeros((tq, tq), F32)
                for bk in range(NUM_BUCKETS - 1):
                    t = jnp.where(bucket == bk, rb_ref[bk, h] - rb_ref[NUM_BUCKETS - 1, h], t)
                btab[h, var] = t * LOG2E

    qi = qi_ref[0]
    qi8 = jnp.concatenate(
        [qi[blk * tq:(blk + 1) * tq, m * LANES:(m + 1) * LANES]
         for blk in range(nb) for m in range(pairs)], axis=0)
    wi = wi_ref[0]
    for blk in range(nb):
        for h in range(H_IDX):
            wib[blk, h] = jnp.broadcast_to(wi[blk * tq:(blk + 1) * tq, h:h + 1], (tq, tk))
    row_q = lax.broadcasted_iota(jnp.int32, (tq, tk), 0)
    lane_k = lax.broadcasted_iota(jnp.int32, (tq, tk), 1)

    def score_tile(kt, carry, *, last):
        k0 = pl.multiple_of(kt * tk, tk)
        da = jnp.maximum(_dot_nt(qi8, kia_ref[0, pl.ds(k0, tk), :]), 0.0)
        db = jnp.maximum(_dot_nt(qi8, kib_ref[0, pl.ds(k0, tk), :]), 0.0)
        for blk in range(nb):
            s = jnp.zeros((tq, tk), F32)
            for m in range(pairs):
                r0 = (blk * pairs + m) * tq
                s = s + wib[blk, 2 * m] * da[r0:r0 + tq]
                s = s + wib[blk, 2 * m + 1] * db[r0:r0 + tq]
            if last:
                s = jnp.where(lane_k <= row_q + blk * tq, s, NEG_INF)
            sc_std[blk, kt] = s
            sc_t[kt, :, blk * tq:(blk + 1) * tq] = s.T
        return carry

    def score_pair(j, carry):
        score_tile(2 * j, carry, last=False)
        return score_tile(2 * j + 1, carry, last=False)

    lax.fori_loop(0, lax.shift_right_logical(step, 1), score_pair, 0)

    @pl.when((step & 1) == 1)
    def _():
        score_tile(step - 1, 0, last=False)

    score_tile(step, 0, last=True)

    st = (SUBLANES, nb * tq)
    p_lane = step * (nb * tq) + lax.broadcasted_iota(jnp.int32, st, 1)
    kq = jnp.minimum(topk, p_lane + 1).astype(F32)

    def search(n):
        def over_tiles(fn, op):
            pair = {jnp.sum: jnp.add, jnp.max: jnp.maximum, jnp.min: jnp.minimum}[op]
            parts = [op(fn(sc_t[kt].reshape(tk // SUBLANES, SUBLANES, nb * tq), kt), axis=0)
                     for kt in range(n)]
            while len(parts) > 1:
                parts = [pair(*parts[i:i + 2]) if i + 1 < len(parts) else parts[i]
                         for i in range(0, len(parts), 2)]
            x = parts[0]
            for s in (4, 2, 1):
                x = pair(x, pltpu.roll(x, s, 0))
            return x

        def count_ge(th):
            return over_tiles(lambda v, kt: jnp.where(v >= th, 1.0, 0.0), jnp.sum)

        mx = over_tiles(lambda v, kt: v, jnp.max)
        lo0 = over_tiles(lambda v, kt: jnp.where(v == NEG_INF, POS_INF, v), jnp.min)

        def bisect(_, c):
            lo, hix, chi = c
            hib = jnp.where(hix == POS_INF, mx, hix)
            mid = 0.5 * lo + 0.5 * hib
            cnt = count_ge(mid)
            ge = cnt >= kq
            return jnp.where(ge, mid, lo), jnp.where(ge, hix, mid), jnp.where(ge, chi, cnt)

        _, hix, chi = lax.fori_loop(
            0, N_BISECT, bisect,
            (lo0, jnp.full(st, POS_INF, F32), jnp.zeros(st, F32)))

        def sweep(cand):
            cs, ms = [], []
            for kt in range(n):
                v = sc_t[kt].reshape(tk // SUBLANES, SUBLANES, nb * tq)
                ge = v >= cand
                cs.append(jnp.sum(jnp.where(ge, 1.0, 0.0), axis=0))
                ms.append(jnp.max(jnp.where(ge, NEG_INF, v), axis=0))
            c, m = functools.reduce(jnp.add, cs), functools.reduce(jnp.maximum, ms)
            for s in (4, 2, 1):
                c = c + pltpu.roll(c, s, 0)
                m = jnp.maximum(m, pltpu.roll(m, s, 0))
            return c, m

        def snap_cond(c):
            return jnp.max(c[5]) > 0.0

        def snap_body(c):
            cand, chi, t, cge, ngt, pending = c
            ct, nxt = sweep(cand)
            act = pending > 0.0
            ok = ct >= kq
            t = jnp.where(act, cand, t)
            cge = jnp.where(act, ct, cge)
            ngt = jnp.where(act, chi, ngt)
            pending = jnp.where(act & ok, 0.0, pending)
            cand = jnp.where(act & ~ok, nxt, cand)
            chi = jnp.where(act & ~ok, ct, chi)
            return cand, chi, t, cge, ngt, pending

        cand0 = over_tiles(lambda v, kt: jnp.where(v < hix, v, NEG_INF), jnp.max)
        zero = jnp.zeros(st, F32)
        first_sweep = snap_body((cand0, chi, zero, zero, zero, jnp.ones(st, F32)))
        _, _, t_k, cge, ngt, _ = lax.while_loop(snap_cond, snap_body, first_sweep)
        thr[...] = t_k
        jthr[...] = jnp.full(st, float(seq), F32)

        need = kq - ngt
        excess = (cge - ngt) > need

        @pl.when(jnp.max(jnp.where(excess, 1.0, 0.0)) > 0.0)
        def _():
            sub_k = lax.broadcasted_iota(jnp.int32, (tk, nb * tq), 0).astype(F32).reshape(
                tk // SUBLANES, SUBLANES, nb * tq)

            def jbisect(_, c):
                jlo, jhi = c
                mid = jnp.floor(0.5 * (jlo + jhi))
                cnt = over_tiles(
                    lambda v, kt: jnp.where((v == t_k) & (sub_k + float(kt * tk) <= mid), 1.0, 0.0),
                    jnp.sum)
                ge = cnt >= need
                return jnp.where(ge, jlo, mid), jnp.where(ge, mid, jhi)

            n_pass = max(1, int(math.ceil(math.log2(seq))))
            _, jhi = lax.fori_loop(
                0, n_pass, jbisect,
                (jnp.full(st, -1.0, F32), jnp.full(st, float(seq - 1), F32)))
            jthr[...] = jhi

    def select_all_causal():
        thr[...] = jnp.full(st, NEG_INF, F32)
        jthr[...] = jnp.full(st, -1.0, F32)

    for n in range(1, seq // tk + 1):
        no_search = n * tk <= topk
        pl.when(n_kt == n)(select_all_causal if no_search else functools.partial(search, n))

    for blk in range(nb):
        for i, ref in enumerate((thr, jthr)):
            v = jnp.concatenate([ref[:, blk * tq:(blk + 1) * tq]] * (tq // SUBLANES), axis=0).T
            tj[blk, i] = jnp.concatenate([v] * (tk // tq), axis=1)

    lane_kf = lane_k.astype(F32)

    def attend_tile(kt, carry=None, *, kind, first=False):
        qall = jnp.concatenate(
            [qlat_ref[0, :, blk * tq:(blk + 1) * tq, :].reshape(hq, D_LAT) for blk in range(nb)],
            axis=0)
        k0 = pl.multiple_of(kt * tk, tk)
        ct = caug_ref[0, pl.ds(k0, tk), :]
        src = _dot_nt(qall, ct[:, :D_LAT])
        kf = lane_kf + (kt * tk).astype(F32)
        for blk in range(nb):
            s = sc_std[blk, kt]
            tb = tj[blk, 0]
            sel = (s > tb) | ((s == tb) & (kf <= tj[blk, 1]))
            ma = jnp.where(sel, 0.0, NEG_INF)
            for h in range(H_ATT):
                r0 = (blk * H_ATT + h) * tq
                lh = src[r0:r0 + tq, :] + ma
                if kind == "near" and blk == 0:
                    lh = jnp.concatenate([lh[:, :tq], lh[:, tq:] + btab[h, 1]], axis=1)
                elif kind == "last" and blk == 0:
                    lh = jnp.concatenate([lh[:, :tq] + btab[h, 0], lh[:, tq:]], axis=1)
                elif kind == "last" and blk == 1:
                    lh = lh + jnp.concatenate([btab[h, 1], btab[h, 0]], axis=1)
                mt = jnp.maximum(lh[:, :tq], lh[:, tq:])
                rm = jnp.broadcast_to(jnp.max(mt, axis=1, keepdims=True), (tq, tq))
                if first:
                    m_new = jnp.maximum(rm, M_INIT)
                else:
                    m_old = mrun[blk, h]
                    m_new = jnp.maximum(m_old, rm)
                    alph[blk, h] = jnp.exp2(m_old - m_new)
                mrun[blk, h] = m_new
                m2 = jnp.concatenate([m_new, m_new], axis=1)
                p_s[r0:r0 + tq, :] = jnp.exp2(lh - m2).astype(BF16)
        pv = _dot(p_s[...], ct)
        if first:
            acc[...] = pv
            return carry
        for blk in range(nb):
            for h in range(H_ATT):
                r0 = (blk * H_ATT + h) * tq
                al = alph[blk, h]
                acc[r0:r0 + tq, :] = acc[r0:r0 + tq, :] * jnp.concatenate([al, al], axis=1) + pv[r0:r0 + tq]
        return carry

    attend_tile(step, kind="last", first=True)

    @pl.when(step >= 1)
    def _():
        attend_tile(step - 1, kind="near")

    lax.fori_loop(0, jnp.maximum(step - 1, 0), functools.partial(attend_tile, kind="far"), 0)

    for m in range(H_ATT // 2):
        rows = []
        for blk in range(nb):
            halves = []
            for h in (2 * m, 2 * m + 1):
                r0 = (blk * H_ATT + h) * tq
                a = acc[r0:r0 + tq, :]
                halves.append(a[:, :D_LAT] * (1.0 / a[:, D_LAT:]))
            rows.append(jnp.concatenate(halves, axis=1))
        pair = jnp.concatenate(rows, axis=0).astype(BF16)
        o_ref[0, :, m * 2 * D_V:(m + 1) * 2 * D_V] = _dot(pair, wuv_ref[m]).astype(BF16)


def _attention(rel_bias, qlat, qi, wi, caug, kia, kib, wuv2, *, topk):
    b, _, s, _ = qlat.shape
    tq, tk = Q_BLOCK, K_TILE
    assert s % tk == 0 and tk == 2 * tq
    nb = tk // tq
    n_kt = s // tk
    hq = H_ATT * tq
    di = H_IDX * D_IDX
    dv = H_ATT * D_V
    qrow = lambda i, j: (i, j, 0)
    full = lambda i, j: (i, 0, 0)
    return pl.pallas_call(
        functools.partial(_attn_body, topk=topk, seq=s),
        grid=(b, s // tk),
        in_specs=[
            pl.BlockSpec(memory_space=pltpu.SMEM),
            pl.BlockSpec((1, H_ATT, tk, D_LAT), lambda i, j: (i, 0, j, 0)),
            pl.BlockSpec((1, tk, di), qrow),
            pl.BlockSpec((1, tk, LANES), qrow),
            pl.BlockSpec((1, s, 2 * D_LAT), full),
            pl.BlockSpec((1, s, LANES), full),
            pl.BlockSpec((1, s, LANES), full),
            _const_spec((H_ATT // 2, 2 * D_LAT, 2 * D_V)),
        ],
        out_specs=pl.BlockSpec((1, tk, dv), qrow),
        out_shape=jax.ShapeDtypeStruct((b, s, dv), BF16),
        scratch_shapes=[
            pltpu.VMEM((nb, n_kt, tq, tk), F32),
            pltpu.VMEM((n_kt, tk, nb * tq), F32),
            pltpu.VMEM((nb * hq, tk), F32),
            pltpu.VMEM((nb, H_ATT, tq, tq), F32),
            pltpu.VMEM((nb, H_ATT, tq, tq), F32),
            pltpu.VMEM((H_ATT, 2, tq, tq), F32),
            pltpu.VMEM((nb, H_IDX, tq, tk), F32),
            pltpu.VMEM((nb * hq, tk), BF16),
            pltpu.VMEM((nb, 2, tq, tk), F32),
            pltpu.VMEM((SUBLANES, nb * tq), F32),
            pltpu.VMEM((SUBLANES, nb * tq), F32),
        ],
        compiler_params=pltpu.CompilerParams(
            dimension_semantics=("arbitrary", "arbitrary"), vmem_limit_bytes=VMEM_LIMIT_BYTES),
        name="dsa_attention",
    )(rel_bias, qlat, qi, wi, caug, kia, kib, wuv2)


def _block_diag_pairs(w):
    h, a, b = w.shape
    w = w.reshape(h // 2, 2, a, b)
    z = jnp.zeros((h // 2, a, b), w.dtype)
    top = jnp.concatenate([w[:, 0], z], axis=2)
    bot = jnp.concatenate([z, w[:, 1]], axis=2)
    return jnp.concatenate([top, bot], axis=1)


def _pack_att_w(w_in):
    assert 2 * D_IDX == LANES and D_IDX + H_IDX <= LANES
    n = w_in.shape[1]
    return jnp.pad(w_in, ((0, 0), (0, -n % LANES)))


def kernel(x, norm_mix_g, norm_mlp_g, final_norm_g, rec_w_in, rec_conv_w, rec_conv_b, rec_w_a, rec_b_a, rec_w_x, rec_b_x, rec_lambda, rec_w_out, att_w_in, att_kv_norm_g, att_w_uk, att_w_uv, att_w_o, rel_bias, mlp_w_up, mlp_w_down):
    b, s, d = x.shape
    depth = norm_mix_g.shape[0]
    topk = min(TOPK_MAX, s // 4)
    row = lambda v: v.reshape(1, -1).astype(F32)
    w_up_all = mlp_w_up.astype(BF16)
    w_dn_all = mlp_w_down.astype(BF16)
    for layer in range(depth):
        j = layer // 2
        g_mix = row(norm_mix_g[layer])
        final = layer == depth - 1
        mlp_w = (row(norm_mlp_g[layer]), w_up_all, w_dn_all, row(final_norm_g))
        if layer % 2 == 0:
            x = _rec_mlp_layer(
                x, g_mix, rec_w_in[j].astype(BF16), rec_conv_w[j].astype(F32), row(rec_conv_b[j]),
                rec_w_a[j].astype(BF16), row(rec_b_a[j]), rec_w_x[j].astype(BF16), row(rec_b_x[j]),
                row(rec_lambda[j]), rec_w_out[j].astype(BF16), *mlp_w, layer=layer, final=final)
        else:
            qlat, caug, qi, kia, kib, wi = _att_proj(
                x, g_mix, _pack_att_w(att_w_in[j].astype(BF16)), row(att_kv_norm_g[j]),
                _block_diag_pairs(att_w_uk[j]).astype(BF16))
            y = _attention(
                rel_bias.astype(F32), qlat, qi, wi, caug, kia, kib,
                _block_diag_pairs(att_w_uv[j]).astype(BF16), topk=topk)
            x = _mlp_layer(
                x.reshape(b * s, d), y.reshape(b * s, d), att_w_o[j].astype(BF16), *mlp_w,
                layer=layer, final=final).reshape(b, s, d)
    return x
```
